```python
import math
import jax, jax.numpy as jnp
from jax import lax
import numpy as np

D_MODEL = 4096
BATCH = 8
SEQ = 2048
DEPTH = 2
DEC_BATCH = 32
DEC_SEQ = 32
PAST_LEN = 2048

CHUNK = 64
Q_BLOCK = 128
NORM_EPS = 1e-5

W_GLA = 3 * D_MODEL // 8
W_RWKV = 3 * D_MODEL // 8
W_SB = D_MODEL - W_GLA - W_RWKV

GLA_HEADS = 6
GLA_DV = W_GLA // GLA_HEADS
GLA_DK = W_GLA // 2 // GLA_HEADS
GLA_QK = GLA_HEADS * GLA_DK
GLA_LOWRANK = 16
GLA_TAU = 16.0

RWKV_HEAD = 64
RWKV_HEADS = W_RWKV // RWKV_HEAD
D_DECAY_LORA = max(32, int(round(1.8 * D_MODEL ** 0.5 / 32)) * 32)
D_AAA_LORA = max(32, int(round(1.8 * D_MODEL ** 0.5 / 32)) * 32)
D_GATE_LORA = max(32, int(round(0.6 * D_MODEL ** 0.8 / 32)) * 32)
RWKV_GN_EPS = 64e-5

SB_HEAD = 128
SB_HEADS = W_SB // SB_HEAD
SB_SCALE = SB_HEAD ** -0.5

D_FF = -(-8 * D_MODEL // (3 * 256)) * 256

GLA_SIZES = [GLA_QK, GLA_QK, W_GLA, W_GLA, GLA_LOWRANK]
RWKV_SIZES = [W_RWKV, D_DECAY_LORA, W_RWKV, W_RWKV, D_AAA_LORA, D_GATE_LORA]
SB_SIZES = [W_SB, W_SB, W_SB]
P_GLA = sum(GLA_SIZES)
P_RWKV = sum(RWKV_SIZES)
P_SB = sum(SB_SIZES)
P_TOTAL = P_GLA + P_RWKV + P_SB

kernel_name = "hybrid_gla_rwkv7_stickbreaking_stream_step"

F32 = jnp.float32


def _split(x, sizes):
    idx = [int(i) for i in np.cumsum(sizes)[:-1]]
    return jnp.split(x, idx, axis=-1)


def _heads(t, n):
    return t.reshape(t.shape[0], t.shape[1], n, -1)


def _rmsnorm(x, g):
    xf = x.astype(F32)
    y = xf * lax.rsqrt(jnp.mean(xf * xf, axis=-1, keepdims=True) + NORM_EPS)
    return y.astype(x.dtype) * g


def _gla_chunked(q, k, v, log_a, S0):
    B, T, H, dk = q.shape
    dv = v.shape[-1]
    L = min(CHUNK, T)
    n = T // L
    tri = jnp.tril(jnp.ones((L, L), dtype=bool))[None, :, :, None, None]

    def chunks(t):
        return t.astype(F32).reshape(B, n, L, H, t.shape[-1]).swapaxes(0, 1)

    def step(S, inp):
        qc, kc, vc, ac = inp
        b = jnp.cumsum(ac, axis=1)
        decay = jnp.where(tri, jnp.exp(jnp.minimum(b[:, :, None] - b[:, None, :], 0.0)), 0.0)
        scores = jnp.einsum("bthd,btshd,bshd->bhts", qc, decay, kc)
        o = (jnp.einsum("bhts,bshv->bthv", scores, vc)
             + jnp.einsum("bthd,bhdv->bthv", qc * jnp.exp(b), S))
        b_last = b[:, -1]
        S = (S * jnp.exp(b_last)[..., None]
             + jnp.einsum("bshd,bshv->bhdv", kc * jnp.exp(b_last[:, None] - b), vc))
        return S, o

    S, o = lax.scan(step, S0.astype(F32), (chunks(q), chunks(k), chunks(v), chunks(log_a)))
    return o.swapaxes(0, 1).reshape(B, T, H, dv), S


def _rwkv7_scan(r, log_w, k, v, kk, a, S0):
    xs = tuple(t.astype(F32).swapaxes(0, 1) for t in (r, log_w, k, v, kk, a))

    def step(S, inp):
        r_t, lw_t, k_t, v_t, kk_t, a_t = inp
        sa = jnp.einsum("bhvk,bhk->bhv", S, -kk_t)
        S = (S * jnp.exp(lw_t)[:, :, None, :]
             + sa[..., None] * (kk_t * a_t)[:, :, None, :]
             + v_t[..., None] * k_t[:, :, None, :])
        return S, jnp.einsum("bhvk,bhk->bhv", S, r_t)

    S, y = lax.scan(step, S0.astype(F32), xs)
    return y.swapaxes(0, 1), S


def _stick_breaking(q, k, v, past):
    B, T, H, d = q.shape
    K = k.shape[1]
    blk = Q_BLOCK if T % Q_BLOCK == 0 else T
    nb = T // blk
    kpos = jnp.arange(K)
    qb = q.reshape(B, nb, blk, H, d).swapaxes(0, 1)
    pb = (past + jnp.arange(T)).reshape(nb, blk)

    def one(args):
        qblk, pblk = args
        z = jnp.einsum("bqhd,bkhd->bhqk", qblk, k).astype(F32) * SB_SCALE
        causal = (kpos[None, :] < pblk[:, None])[None, None]
        log_beta = jax.nn.log_sigmoid(z)
        log_1m = jnp.where(causal, jax.nn.log_sigmoid(-z), 0.0)
        rest = lax.cumsum(log_1m, axis=3, reverse=True) - log_1m
        att = jnp.where(causal, jnp.exp(log_beta + rest), 0.0)
        return jnp.einsum("bhqk,bkhd->bqhd", att.astype(v.dtype), v)

    o = lax.map(one, (qb, pb))
    return o.swapaxes(0, 1).reshape(B, T, H, d)


def _layer(x, S_gla, S_rwkv, shift_prev, k_past, v_past, lw):
    B, T, _ = x.shape
    dt = x.dtype
    h = _rmsnorm(x, lw["norm1_g"])
    proj = h @ lw["w_in"]
    p_gla, p_rwkv, p_sb = _split(proj, [P_GLA, P_RWKV, P_SB])

    g_q, g_k, g_v, g_g, g_a = _split(p_gla, GLA_SIZES)
    log_a = jax.nn.log_sigmoid((g_a @ lw["gla_wa2"] + lw["gla_ba"]).astype(F32)) / GLA_TAU
    o_gla, S_gla_new = _gla_chunked(_heads(g_q * GLA_DK ** -0.5, GLA_HEADS), _heads(g_k, GLA_HEADS),
                                    _heads(g_v, GLA_HEADS), _heads(log_a, GLA_HEADS), S_gla)
    o_gla = _rmsnorm(o_gla.astype(dt), lw["gla_norm_g"]).reshape(B, T, W_GLA) * jax.nn.silu(g_g)

    prev = jnp.concatenate([shift_prev.astype(dt), p_rwkv[:, :-1]], axis=1)
    pm = p_rwkv + (prev - p_rwkv) * lw["rwkv_mu"]
    r, w_low, k, v, a_low, g_low = _split(pm, RWKV_SIZES)
    log_w = -math.exp(-0.5) * jax.nn.sigmoid((lw["rwkv_w0"] + jnp.tanh(w_low) @ lw["rwkv_w2"]).astype(F32))
    a = jax.nn.sigmoid(lw["rwkv_a0"] + a_low @ lw["rwkv_a2"])
    g = jax.nn.sigmoid(g_low) @ lw["rwkv_g2"]
    kk = _heads(k * lw["rwkv_k_k"], RWKV_HEADS).astype(F32)
    kk = kk / jnp.maximum(jnp.sqrt(jnp.sum(kk * kk, axis=-1, keepdims=True)), 1e-12)
    k = k * (1.0 + (a - 1.0) * lw["rwkv_k_a"])
    rh, kh, vh = _heads(r, RWKV_HEADS), _heads(k, RWKV_HEADS), _heads(v, RWKV_HEADS)
    y, S_rwkv_new = _rwkv7_scan(rh, _heads(log_w, RWKV_HEADS), kh, vh, kk, _heads(a, RWKV_HEADS), S_rwkv)
    mean = jnp.mean(y, axis=-1, keepdims=True)
    var = jnp.mean(jnp.square(y - mean), axis=-1, keepdims=True)
    y = ((y - mean) * lax.rsqrt(var + RWKV_GN_EPS)).reshape(B, T, W_RWKV) * lw["rwkv_ln_g"] + lw["rwkv_ln_b"]
    bonus = jnp.sum((rh * kh * lw["rwkv_r_k"]).astype(F32), axis=-1, keepdims=True) * vh.astype(F32)
    o_rwkv = ((y + bonus.reshape(B, T, W_RWKV)) * g.astype(F32)).astype(dt)

    s_q, s_k, s_v = _split(p_sb, SB_SIZES)
    qh, kn, vn = _heads(s_q, SB_HEADS), _heads(s_k, SB_HEADS), _heads(s_v, SB_HEADS)
    k_all = jnp.concatenate([k_past.astype(dt), kn], axis=1)
    v_all = jnp.concatenate([v_past.astype(dt), vn], axis=1)
    o_sb = _stick_breaking(qh, k_all, v_all, k_past.shape[1])
    o_sb = _rmsnorm(o_sb, lw["sb_norm_g"].reshape(SB_HEADS, SB_HEAD)).reshape(B, T, W_SB)

    x = x + jnp.concatenate([o_gla, o_rwkv, o_sb], axis=-1) @ lw["w_out"]

    h2 = _rmsnorm(x, lw["norm2_g"])
    x = x + (jax.nn.silu(h2 @ lw["w_gate"]) * (h2 @ lw["w_up"])) @ lw["w_down"]
    return x, S_gla_new, S_rwkv_new, p_rwkv[:, -1:], kn, vn


def setup_inputs(seed: int = 0) -> dict:
    key = jax.random.key(seed)
    ks = iter(jax.random.split(key, 40))

    def nrm(shape, scale):
        return jax.random.normal(next(ks), shape, F32) * scale

    def gain(shape):
        return 1.0 + nrm(shape, 0.02)

    return {
        "x_prompt": nrm((BATCH, SEQ, D_MODEL), 1.0),
        "x_sample": nrm((DEC_BATCH, DEC_SEQ, D_MODEL), 1.0),
        "state_gla": nrm((DEPTH, DEC_BATCH, GLA_HEADS, GLA_DK, GLA_DV), 0.5),
        "state_rwkv": nrm((DEPTH, DEC_BATCH, RWKV_HEADS, RWKV_HEAD, RWKV_HEAD), 0.3),
        "state_rwkv_shift": nrm((DEPTH, DEC_BATCH, 1, P_RWKV), 1.0),
        "cache_sb_k": nrm((DEPTH, DEC_BATCH, PAST_LEN, SB_HEADS, SB_HEAD), 1.0),
        "cache_sb_v": nrm((DEPTH, DEC_BATCH, PAST_LEN, SB_HEADS, SB_HEAD), 1.0),
        "norm1_g": gain((DEPTH, D_MODEL)),
        "w_in": nrm((DEPTH, D_MODEL, P_TOTAL), D_MODEL ** -0.5),
        "gla_wa2": nrm((DEPTH, GLA_LOWRANK, GLA_QK), GLA_LOWRANK ** -0.5),
        "gla_ba": nrm((DEPTH, GLA_QK), 0.1),
        "gla_norm_g": gain((DEPTH, GLA_DV)),
        "rwkv_mu": jax.random.uniform(next(ks), (DEPTH, P_RWKV), F32),
        "rwkv_w0": nrm((DEPTH, W_RWKV), 0.5),
        "rwkv_w2": nrm((DEPTH, D_DECAY_LORA, W_RWKV), D_DECAY_LORA ** -0.5),
        "rwkv_a0": nrm((DEPTH, W_RWKV), 0.1),
        "rwkv_a2": nrm((DEPTH, D_AAA_LORA, W_RWKV), D_AAA_LORA ** -0.5),
        "rwkv_g2": nrm((DEPTH, D_GATE_LORA, W_RWKV), D_GATE_LORA ** -0.5),
        "rwkv_k_k": 0.85 + nrm((DEPTH, W_RWKV), 0.02),
        "rwkv_k_a": 1.0 + nrm((DEPTH, W_RWKV), 0.02),
        "rwkv_r_k": nrm((DEPTH, RWKV_HEADS, RWKV_HEAD), 0.1),
        "rwkv_ln_g": gain((DEPTH, W_RWKV)),
        "rwkv_ln_b": nrm((DEPTH, W_RWKV), 0.02),
        "sb_norm_g": gain((DEPTH, W_SB)),
        "w_out": nrm((DEPTH, D_MODEL, D_MODEL), D_MODEL ** -0.5),
        "norm2_g": gain((DEPTH, D_MODEL)),
        "w_gate": nrm((DEPTH, D_MODEL, D_FF), D_MODEL ** -0.5),
        "w_up": nrm((DEPTH, D_MODEL, D_FF), D_MODEL ** -0.5),
        "w_down": nrm((DEPTH, D_FF, D_MODEL), D_FF ** -0.5),
        "final_g": gain((D_MODEL,)),
    }


def reference(x_prompt, x_sample, state_gla, state_rwkv, state_rwkv_shift, cache_sb_k, cache_sb_v,
              norm1_g, w_in, gla_wa2, gla_ba, gla_norm_g, rwkv_mu, rwkv_w0, rwkv_w2, rwkv_a0, rwkv_a2,
              rwkv_g2, rwkv_k_k, rwkv_k_a, rwkv_r_k, rwkv_ln_g, rwkv_ln_b, sb_norm_g, w_out, norm2_g,
              w_gate, w_up, w_down, final_g):
    dt = x_prompt.dtype
    B = x_prompt.shape[0]
    gla0 = jnp.zeros((B, GLA_HEADS, GLA_DK, GLA_DV), F32)
    rwkv0 = jnp.zeros((B, RWKV_HEADS, RWKV_HEAD, RWKV_HEAD), F32)
    shift0 = jnp.zeros((B, 1, P_RWKV), dt)
    kv0 = jnp.zeros((B, 0, SB_HEADS, SB_HEAD), dt)

    xp, xs = x_prompt, x_sample
    gla_p, gla_s, rw_p, rw_s, sh_p, sh_s, kp, ksm, vp, vsm = [], [], [], [], [], [], [], [], [], []
    for l in range(DEPTH):
        lw = dict(norm1_g=norm1_g[l], w_in=w_in[l], gla_wa2=gla_wa2[l], gla_ba=gla_ba[l],
                  gla_norm_g=gla_norm_g[l], rwkv_mu=rwkv_mu[l], rwkv_w0=rwkv_w0[l], rwkv_w2=rwkv_w2[l],
                  rwkv_a0=rwkv_a0[l], rwkv_a2=rwkv_a2[l], rwkv_g2=rwkv_g2[l], rwkv_k_k=rwkv_k_k[l],
                  rwkv_k_a=rwkv_k_a[l], rwkv_r_k=rwkv_r_k[l], rwkv_ln_g=rwkv_ln_g[l], rwkv_ln_b=rwkv_ln_b[l],
                  sb_norm_g=sb_norm_g[l], w_out=w_out[l], norm2_g=norm2_g[l], w_gate=w_gate[l],
                  w_up=w_up[l], w_down=w_down[l])
        xp, g1, r1, s1, k1, v1 = _layer(xp, gla0, rwkv0, shift0, kv0, kv0, lw)
        xs, g2, r2, s2, k2, v2 = _layer(xs, state_gla[l], state_rwkv[l], state_rwkv_shift[l],
                                        cache_sb_k[l], cache_sb_v[l], lw)
        gla_p.append(g1); gla_s.append(g2); rw_p.append(r1); rw_s.append(r2)
        sh_p.append(s1); sh_s.append(s2); kp.append(k1); ksm.append(k2); vp.append(v1); vsm.append(v2)

    y_prompt = _rmsnorm(xp, final_g)
    y_sample = _rmsnorm(xs, final_g)
    sd = state_gla.dtype
    return (y_prompt, y_sample,
            jnp.stack(gla_p).astype(sd), jnp.stack(gla_s).astype(sd),
            jnp.stack(rw_p).astype(sd), jnp.stack(rw_s).astype(sd),
            jnp.stack(sh_p), jnp.stack(sh_s),
            jnp.stack(kp), jnp.stack(ksm),
            jnp.stack(vp), jnp.stack(vsm))
```

```python
import functools
import math

import jax
import jax.numpy as jnp
from jax import lax
from jax.experimental import pallas as pl
from jax.experimental.pallas import tpu as pltpu

F32 = jnp.float32
BF16 = jnp.bfloat16

D_MODEL = 4096
NORM_EPS = 1e-5
W_GLA = 1536
W_RWKV = 1536
W_SB = 1024
GLA_HEADS = 6
GLA_DV = 256
GLA_DK = 128
GLA_QK = 768
GLA_LOWRANK = 16
GLA_TAU = 16.0
GLA_CHUNK = 64
RWKV_HEAD = 64
RWKV_HEADS = 24
D_LORA = 128
D_GATE_LORA = 480
D_GATE_PAD = 512
RWKV_GN_EPS = 64e-5
SB_HEAD = 128
SB_HEADS = 8
SB_SCALE = SB_HEAD ** -0.5
SB_BLOCK = 128
D_FF = 11008
D_FF_PAD = 11264
P_GLA = 4624
P_RWKV = 5344
P_RWKV_PAD = 5376
P_SB = 3072

COL_RWKV = 0
COL_SB_Q = P_RWKV_PAD
COL_SB_K = COL_SB_Q + W_SB
COL_SB_V = COL_SB_K + W_SB
COL_GLA_V = COL_SB_V + W_SB
COL_GLA_G = COL_GLA_V + W_GLA
COL_GLA_Q = COL_GLA_G + W_GLA
COL_GLA_K = COL_GLA_Q + GLA_QK
COL_GLA_A = COL_GLA_K + GLA_QK
P_PAD = 13312

LANE = 128
SUBLANE = 8
VMEM_LIMIT = 56 * 1024 * 1024

EXP_ZERO_BELOW = -104.0


def _tile(n, candidates):
    for c in candidates:
        if n % c == 0:
            return c
    return n


def _params(sem, vmem=None):
    return pltpu.CompilerParams(dimension_semantics=sem, vmem_limit_bytes=vmem)


def _rmsnorm_kernel(x_ref, g_ref, o_ref):
    x = x_ref[...]
    ms = jnp.mean(x * x, axis=-1, keepdims=True)
    o_ref[...] = (x * lax.rsqrt(ms + NORM_EPS) * g_ref[...]).astype(o_ref.dtype)


def _rmsnorm(x, g, out_dtype):
    m, d = x.shape
    tm = _tile(m, (512, 256, 128, 64, 32, 16, 8))
    return pl.pallas_call(
        _rmsnorm_kernel,
        grid=(m // tm,),
        in_specs=[pl.BlockSpec((tm, d), lambda i: (i, 0)), pl.BlockSpec((1, d), lambda i: (0, 0))],
        out_specs=pl.BlockSpec((tm, d), lambda i: (i, 0)),
        out_shape=jax.ShapeDtypeStruct((m, d), out_dtype),
        compiler_params=_params(("parallel",), VMEM_LIMIT),
        name="rmsnorm",
    )(x, g.reshape(1, d))


def _mm_kernel(x_ref, w_ref, o_ref):
    o_ref[...] = jnp.dot(x_ref[...], w_ref[...], preferred_element_type=F32).astype(o_ref.dtype)


def _mm_res_kernel(x_ref, w_ref, r_ref, o_ref):
    acc = jnp.dot(x_ref[...], w_ref[...], preferred_element_type=F32)
    k = pl.program_id(2)

    @pl.when(k == 0)
    def _():
        o_ref[...] = r_ref[...] + acc

    @pl.when(k != 0)
    def _():
        o_ref[...] += acc


def _swiglu_kernel(x_ref, wg_ref, wu_ref, o_ref):
    x = x_ref[...]
    g = jnp.dot(x, wg_ref[...], preferred_element_type=F32)
    u = jnp.dot(x, wu_ref[...], preferred_element_type=F32)
    o_ref[...] = (g * jax.nn.sigmoid(g) * u).astype(o_ref.dtype)


def _matmul(x, w, out_dtype):
    m, k = x.shape
    n = w.shape[1]
    tm = _tile(m, (1024, 512, 256, 128, 64, 32, 16, 8))
    tn = _tile(n, (1024, 512, 256, 128))
    return pl.pallas_call(
        _mm_kernel,
        grid=(m // tm, n // tn),
        in_specs=[pl.BlockSpec((tm, k), lambda i, j: (i, 0)), pl.BlockSpec((k, tn), lambda i, j: (0, j))],
        out_specs=pl.BlockSpec((tm, tn), lambda i, j: (i, j)),
        out_shape=jax.ShapeDtypeStruct((m, n), out_dtype),
        compiler_params=_params(("parallel", "parallel"), VMEM_LIMIT),
        name="matmul",
    )(x, w)


def _matmul_residual(x, w, res, tk):
    m, k = x.shape
    n = w.shape[1]
    tm = _tile(m, (1024, 512, 256, 128, 64, 32, 16, 8))
    tn = _tile(n, (1024, 512, 256, 128))
    return pl.pallas_call(
        _mm_res_kernel,
        grid=(m // tm, n // tn, k // tk),
        in_specs=[pl.BlockSpec((tm, tk), lambda i, j, kk: (i, kk)),
                  pl.BlockSpec((tk, tn), lambda i, j, kk: (kk, j)),
                  pl.BlockSpec((tm, tn), lambda i, j, kk: (i, j))],
        out_specs=pl.BlockSpec((tm, tn), lambda i, j, kk: (i, j)),
        out_shape=jax.ShapeDtypeStruct((m, n), F32),
        compiler_params=_params(("parallel", "parallel", "arbitrary"), VMEM_LIMIT),
        name="matmul_residual",
    )(x, w, res)


def _swiglu(x, wg, wu):
    m, k = x.shape
    n = wg.shape[1]
    tm = _tile(m, (1024, 512, 256, 128, 64, 32, 16, 8))
    tn = _tile(n, (512, 256, 128))
    return pl.pallas_call(
        _swiglu_kernel,
        grid=(m // tm, n // tn),
        in_specs=[pl.BlockSpec((tm, k), lambda i, j: (i, 0)),
                  pl.BlockSpec((k, tn), lambda i, j: (0, j)),
                  pl.BlockSpec((k, tn), lambda i, j: (0, j))],
        out_specs=pl.BlockSpec((tm, tn), lambda i, j: (i, j)),
        out_shape=jax.ShapeDtypeStruct((m, n), BF16),
        compiler_params=_params(("parallel", "parallel"), VMEM_LIMIT),
        name="swiglu",
    )(x, wg, wu)


def _log_sigmoid(x):
    return jnp.minimum(x, 0.0) - jnp.log1p(jnp.exp(-jnp.abs(x)))


def _split_bf16(x):
    hi = x.astype(BF16)
    lo = (x - hi.astype(F32)).astype(BF16)
    return hi, lo


def _dot_nt(a, b):
    return lax.dot_general(a, b, (((1,), (1,)), ((), ())), preferred_element_type=F32)


def _dot_tn(a, b):
    return lax.dot_general(a, b, (((0,), (0,)), ((), ())), preferred_element_type=F32)


def _gla_kernel(q_ref, k_ref, v_ref, g_ref, al_ref, wa_ref, ba_ref, ng_ref, s0_ref, o_ref, sT_ref, s_scr,
                *, chunk, n_chunks, n_steps):
    c = pl.program_id(2)

    @pl.when(c == 0)
    def _():
        s_scr[...] = s0_ref[...]

    row = lax.broadcasted_iota(jnp.int32, (chunk, chunk), 0)
    col = lax.broadcasted_iota(jnp.int32, (chunk, chunk), 1)
    lower = col <= row
    tri = jnp.where(lower, 1.0, 0.0).astype(BF16)
    wa = wa_ref[...]
    ba = ba_ref[...]
    ng = ng_ref[...]

    for ci in range(n_chunks):
        rows = pl.ds(ci * chunk, chunk)
        x = jnp.dot(al_ref[rows, :].astype(BF16), wa, preferred_element_type=F32) + ba
        log_a = _log_sigmoid(x) * (1.0 / GLA_TAU)
        hi, lo = _split_bf16(log_a)
        b = jnp.dot(tri, hi, preferred_element_type=F32) + jnp.dot(tri, lo, preferred_element_type=F32)
        b_last = b[chunk - 1:chunk, :]
        q = q_ref[rows, :] * (GLA_DK ** -0.5)
        k = k_ref[rows, :]
        v = v_ref[rows, :].astype(BF16)
        qd = (q * jnp.exp(b)).astype(BF16)
        kd = (k * jnp.exp(-b)).astype(BF16)
        scores = jnp.where(lower, _dot_nt(qd, kd), 0.0).astype(BF16)
        s_t = s_scr[...]
        o = jnp.dot(scores, v, preferred_element_type=F32) + _dot_nt(qd, s_t.astype(BF16))
        kdec = (k * jnp.exp(b_last - b)).astype(BF16)
        s_scr[...] = s_t * jnp.exp(b_last) + _dot_tn(v, kdec)
        ms = jnp.mean(o * o, axis=-1, keepdims=True)
        gg = g_ref[rows, :]
        o_ref[rows, :] = (o * lax.rsqrt(ms + NORM_EPS) * ng * (gg * jax.nn.sigmoid(gg))).astype(o_ref.dtype)

    @pl.when(c == n_steps - 1)
    def _():
        sT_ref[...] = s_scr[...]


def _gla(proj, row0, batch, seq, wa_pad, ba, norm_g, s0_t):
    chunk = min(GLA_CHUNK, seq)
    tb = _tile(seq, (256, 128, 64, 32, 16, 8))
    n_chunks = tb // chunk
    n_steps = seq // tb
    rb0 = row0 // tb

    def rows(b, c):
        return rb0 + b * n_steps + c

    kern = functools.partial(_gla_kernel, chunk=chunk, n_chunks=n_chunks, n_steps=n_steps)
    return pl.pallas_call(
        kern,
        grid=(batch, GLA_HEADS, n_steps),
        in_specs=[
            pl.BlockSpec((tb, GLA_DK), lambda b, h, c: (rows(b, c), COL_GLA_Q // GLA_DK + h)),
            pl.BlockSpec((tb, GLA_DK), lambda b, h, c: (rows(b, c), COL_GLA_K // GLA_DK + h)),
            pl.BlockSpec((tb, GLA_DV), lambda b, h, c: (rows(b, c), COL_GLA_V // GLA_DV + h)),
            pl.BlockSpec((tb, GLA_DV), lambda b, h, c: (rows(b, c), COL_GLA_G // GLA_DV + h)),
            pl.BlockSpec((tb, LANE), lambda b, h, c: (rows(b, c), COL_GLA_A // LANE)),
            pl.BlockSpec((LANE, GLA_DK), lambda b, h, c: (0, h)),
            pl.BlockSpec((1, GLA_DK), lambda b, h, c: (0, h)),
            pl.BlockSpec((1, GLA_DV), lambda b, h, c: (0, 0)),
            pl.BlockSpec((None, None, GLA_DV, GLA_DK), lambda b, h, c: (b, h, 0, 0)),
        ],
        out_specs=[
            pl.BlockSpec((tb, GLA_DV), lambda b, h, c: (b * n_steps + c, h)),
            pl.BlockSpec((None, None, GLA_DV, GLA_DK), lambda b, h, c: (b, h, 0, 0)),
        ],
        out_shape=[
            jax.ShapeDtypeStruct((batch * seq, W_GLA), BF16),
            jax.ShapeDtypeStruct((batch, GLA_HEADS, GLA_DV, GLA_DK), F32),
        ],
        scratch_shapes=[pltpu.VMEM((GLA_DV, GLA_DK), F32)],
        compiler_params=_params(("parallel", "parallel", "arbitrary"), VMEM_LIMIT),
        name="gla",
    )(proj, proj, proj, proj, proj, wa_pad, ba, norm_g, s0_t)


def _sb_consts():
    row = lax.broadcasted_iota(jnp.int32, (SB_BLOCK, SB_BLOCK), 0)
    col = lax.broadcasted_iota(jnp.int32, (SB_BLOCK, SB_BLOCK), 1)
    r2 = lax.broadcasted_iota(jnp.int32, (SB_BLOCK, 2 * SB_BLOCK), 0)
    c2 = lax.broadcasted_iota(jnp.int32, (SB_BLOCK, 2 * SB_BLOCK), 1)
    suffix_and_total = jnp.where((r2 > c2) | (c2 >= SB_BLOCK), 1.0, 0.0).astype(BF16)
    return row, col, suffix_and_total


def _sb_block(q, kblk, vblk, run, acc, suffix_and_total, causal):
    z = _dot_nt(q, kblk) * SB_SCALE
    sp = jnp.maximum(z, 0.0) + jnp.log1p(jnp.exp(-jnp.abs(z)))
    log_1m = -sp
    if causal is not None:
        log_1m = jnp.where(causal, log_1m, 0.0)
    log_beta = z - sp
    hi, lo = _split_bf16(log_1m)
    cs = (jnp.dot(hi, suffix_and_total, preferred_element_type=F32)
          + jnp.dot(lo, suffix_and_total, preferred_element_type=F32))
    att = jnp.exp(log_beta + cs[:, :SB_BLOCK] + run)
    if causal is not None:
        att = jnp.where(causal, att, 0.0)
    acc = acc + jnp.dot(att.astype(BF16), vblk, preferred_element_type=F32)
    return run + cs[:, SB_BLOCK:], acc


def _sb_sweep(q, k_ref, v_ref, n_blocks, run, acc, suffix_and_total):
    def cond(carry):
        i, _, _, run_max = carry
        return jnp.logical_and(i < n_blocks, run_max > EXP_ZERO_BELOW)

    def body(carry):
        i, run, acc, _ = carry
        start = pl.multiple_of((n_blocks - 1 - i) * SB_BLOCK, SB_BLOCK)
        kblk = k_ref[pl.ds(start, SB_BLOCK), :].astype(BF16)
        vblk = v_ref[pl.ds(start, SB_BLOCK), :].astype(BF16)
        run, acc = _sb_block(q, kblk, vblk, run, acc, suffix_and_total, None)
        return i + 1, run, acc, jnp.max(run)

    _, _, acc, _ = lax.while_loop(cond, body, (jnp.int32(0), run, acc, jnp.max(run)))
    return acc


def _sb_finish(acc, g_ref, o_ref):
    ms = jnp.mean(acc * acc, axis=-1, keepdims=True)
    o_ref[...] = (acc * lax.rsqrt(ms + NORM_EPS) * g_ref[...]).astype(o_ref.dtype)


def _sb_prompt_kernel(q_ref, k_ref, v_ref, g_ref, o_ref):
    qi = pl.program_id(2)
    row, col, suffix_and_total = _sb_consts()
    q = q_ref[...].astype(BF16)
    start = pl.multiple_of(qi * SB_BLOCK, SB_BLOCK)
    kblk = k_ref[pl.ds(start, SB_BLOCK), :].astype(BF16)
    vblk = v_ref[pl.ds(start, SB_BLOCK), :].astype(BF16)
    zero = jnp.zeros((SB_BLOCK, SB_BLOCK), F32)
    run, acc = _sb_block(q, kblk, vblk, zero, zero, suffix_and_total, col < row)
    acc = _sb_sweep(q, k_ref, v_ref, qi, run, acc, suffix_and_total)
    _sb_finish(acc, g_ref, o_ref)


def _sb_prompt(proj, batch, seq, norm_g):
    nq = seq // SB_BLOCK
    return pl.pallas_call(
        _sb_prompt_kernel,
        grid=(batch, SB_HEADS, nq),
        in_specs=[
            pl.BlockSpec((SB_BLOCK, SB_HEAD), lambda b, h, i: (b * nq + i, COL_SB_Q // SB_HEAD + h)),
            pl.BlockSpec((seq, SB_HEAD), lambda b, h, i: (b, COL_SB_K // SB_HEAD + h)),
            pl.BlockSpec((seq, SB_HEAD), lambda b, h, i: (b, COL_SB_V // SB_HEAD + h)),
            pl.BlockSpec((1, SB_HEAD), lambda b, h, i: (0, h)),
        ],
        out_specs=pl.BlockSpec((SB_BLOCK, SB_HEAD), lambda b, h, i: (b * nq + i, h)),
        out_shape=jax.ShapeDtypeStruct((batch * seq, W_SB), BF16),
        compiler_params=_params(("parallel", "parallel", "arbitrary"), VMEM_LIMIT),
        name="sb_prompt",
    )(proj, proj, proj, norm_g)


def _sb_sample_kernel(q_ref, k_ref, v_ref, kp_ref, vp_ref, g_ref, o_ref, *, seq, n_past_blocks):
    row, col, suffix_and_total = _sb_consts()
    pad = jnp.zeros((SB_BLOCK - seq, SB_HEAD), BF16)
    q = jnp.concatenate([q_ref[...].astype(BF16), pad], axis=0)
    kblk = jnp.concatenate([k_ref[...].astype(BF16), pad], axis=0)
    vblk = jnp.concatenate([v_ref[...].astype(BF16), pad], axis=0)
    zero = jnp.zeros((SB_BLOCK, SB_BLOCK), F32)
    run, acc = _sb_block(q, kblk, vblk, zero, zero, suffix_and_total, (col < row) & (col < seq))
    acc = _sb_sweep(q, kp_ref, vp_ref, n_past_blocks, run, acc, suffix_and_total)
    ms = jnp.mean(acc * acc, axis=-1, keepdims=True)
    out = acc * lax.rsqrt(ms + NORM_EPS) * g_ref[...]
    o_ref[...] = out[:seq, :].astype(o_ref.dtype)


def _sb_sample(proj, row0, batch, seq, k_past, v_past, norm_g):
    past = k_past.shape[1]
    rb0 = row0 // seq
    kern = functools.partial(_sb_sample_kernel, seq=seq, n_past_blocks=past // SB_BLOCK)
    return pl.pallas_call(
        kern,
        grid=(batch, SB_HEADS),
        in_specs=[
            pl.BlockSpec((seq, SB_HEAD), lambda b, h: (rb0 + b, COL_SB_Q // SB_HEAD + h)),
            pl.BlockSpec((seq, SB_HEAD), lambda b, h: (rb0 + b, COL_SB_K // SB_HEAD + h)),
            pl.BlockSpec((seq, SB_HEAD), lambda b, h: (rb0 + b, COL_SB_V // SB_HEAD + h)),
            pl.BlockSpec((None, past, SB_HEAD), lambda b, h: (b, 0, h)),
            pl.BlockSpec((None, past, SB_HEAD), lambda b, h: (b, 0, h)),
            pl.BlockSpec((1, SB_HEAD), lambda b, h: (0, h)),
        ],
        out_specs=pl.BlockSpec((seq, SB_HEAD), lambda b, h: (b, h)),
        out_shape=jax.ShapeDtypeStruct((batch * seq, W_SB), BF16),
        compiler_params=_params(("parallel", "parallel"), VMEM_LIMIT),
        name="sb_sample",
    )(proj, proj, proj, k_past.reshape(batch, past, W_SB), v_past.reshape(batch, past, W_SB), norm_g)


def _rwkv_prep_kernel(p_ref, prev_ref, mu_ref, w0_ref, a0_ref, kk_ref, ka_ref, w2_ref, a2_ref, g2_ref,
                      r_out, w_out, k_out, v_out, kk_out, a_out, g_out):
    p = p_ref[...]
    pm = p + (prev_ref[...] - p) * mu_ref[...]
    o = 0
    r = pm[:, o:o + W_RWKV]; o += W_RWKV
    w_low = pm[:, o:o + D_LORA]; o += D_LORA
    k = pm[:, o:o + W_RWKV]; o += W_RWKV
    v = pm[:, o:o + W_RWKV]; o += W_RWKV
    a_low = pm[:, o:o + D_LORA]; o += D_LORA
    g_low = pm[:, o:o + D_GATE_PAD]
    log_w = -math.exp(-0.5) * jax.nn.sigmoid(
        w0_ref[...] + jnp.dot(jnp.tanh(w_low).astype(BF16), w2_ref[...], preferred_element_type=F32))
    a = jax.nn.sigmoid(a0_ref[...] + jnp.dot(a_low.astype(BF16), a2_ref[...], preferred_element_type=F32))
    g = jnp.dot(jax.nn.sigmoid(g_low).astype(BF16), g2_ref[...], preferred_element_type=F32)
    r_out[...] = r
    w_out[...] = jnp.exp(log_w)
    k_out[...] = k * (1.0 + (a - 1.0) * ka_ref[...])
    v_out[...] = v
    kk_out[...] = k * kk_ref[...]
    a_out[...] = a
    g_out[...] = g


def _rwkv_prep(proj, prev, mu, w0, a0, k_k, k_a, w2, a2, g2):
    m = proj.shape[0]
    tm = _tile(m, (128, 64, 32, 16, 8))
    row = lambda i: (i, 0)
    fixed = lambda i: (0, 0)
    vec = pl.BlockSpec((1, W_RWKV), fixed)
    out = pl.BlockSpec((tm, W_RWKV), row)
    return pl.pallas_call(
        _rwkv_prep_kernel,
        grid=(m // tm,),
        in_specs=[pl.BlockSpec((tm, P_RWKV_PAD), row), pl.BlockSpec((tm, P_RWKV_PAD), row),
                  pl.BlockSpec((1, P_RWKV_PAD), fixed), vec, vec, vec, vec,
                  pl.BlockSpec((D_LORA, W_RWKV), fixed), pl.BlockSpec((D_LORA, W_RWKV), fixed),
                  pl.BlockSpec((D_GATE_PAD, W_RWKV), fixed)],
        out_specs=[out] * 7,
        out_shape=[jax.ShapeDtypeStruct((m, W_RWKV), F32)] * 7,
        compiler_params=_params(("parallel",), VMEM_LIMIT),
        name="rwkv_prep",
    )(proj, prev, mu, w0, a0, k_k, k_a, w2, a2, g2)


def _rwkv_scan_kernel(r_ref, w_ref, k_ref, v_ref, kk_ref, a_ref, rk_ref, lng_ref, lnb_ref, s0_ref,
                      z_ref, sT_ref, s_scr, kk_scr, b_scr, y_scr, *, t_block, n_steps, lanes):
    c = pl.program_id(1)
    n = RWKV_HEAD

    @pl.when(c == 0)
    def _():
        s_scr[...] = s0_ref[...]

    def step(t, carry):
        kk_raw = kk_ref[t]
        norm = jnp.sqrt(jnp.sum(kk_raw * kk_raw, axis=0, keepdims=True))
        kk = kk_raw / jnp.maximum(norm, 1e-12)
        kk_scr[...] = kk
        b_scr[...] = kk * a_ref[t]

        for lg in range(lanes // LANE):
            ls = pl.ds(lg * LANE, LANE)

            def value_rows(vo, carry2):
                v0 = pl.multiple_of(vo * SUBLANE, SUBLANE)
                v_tile = v_ref[t, pl.ds(v0, SUBLANE), ls]
                y_rows = []
                for vi in range(SUBLANE):
                    s_v = s_scr[v0 + vi, :, ls]
                    sa = jnp.sum(s_v * kk_scr[:, ls], axis=0, keepdims=True)
                    s_new = s_v * w_ref[t, :, ls] - sa * b_scr[:, ls] + v_tile[vi:vi + 1, :] * k_ref[t, :, ls]
                    s_scr[v0 + vi, :, ls] = s_new
                    y_rows.append(jnp.sum(s_new * r_ref[t, :, ls], axis=0, keepdims=True))
                y_scr[pl.ds(v0, SUBLANE), ls] = jnp.concatenate(y_rows, axis=0)
                return carry2

            lax.fori_loop(0, n // SUBLANE, value_rows, 0)

        y = y_scr[...]
        mean = jnp.mean(y, axis=0, keepdims=True)
        d = y - mean
        var = jnp.mean(d * d, axis=0, keepdims=True)
        bonus = jnp.sum(r_ref[t] * k_ref[t] * rk_ref[...], axis=0, keepdims=True) * v_ref[t]
        z_ref[t] = d * lax.rsqrt(var + RWKV_GN_EPS) * lng_ref[...] + lnb_ref[...] + bonus
        return carry

    lax.fori_loop(0, t_block, step, 0)

    @pl.when(c == n_steps - 1)
    def _():
        sT_ref[...] = s_scr[...]


def _rwkv_scan(r, w, k, v, kk, a, rk, lng, lnb, s0):
    seq, n, lanes = r.shape
    t_block = _tile(seq, (8, 4, 2, 1))
    n_steps = seq // t_block
    lb = _tile(lanes, (2 * LANE, LANE))
    seq_spec = pl.BlockSpec((t_block, n, lb), lambda g, c: (c, 0, g))
    vec_spec = pl.BlockSpec((n, lb), lambda g, c: (0, g))
    st_spec = pl.BlockSpec((n, n, lb), lambda g, c: (0, 0, g))
    kern = functools.partial(_rwkv_scan_kernel, t_block=t_block, n_steps=n_steps, lanes=lb)
    return pl.pallas_call(
        kern,
        grid=(lanes // lb, n_steps),
        in_specs=[seq_spec] * 6 + [vec_spec] * 3 + [st_spec],
        out_specs=[seq_spec, st_spec],
        out_shape=[jax.ShapeDtypeStruct((seq, n, lanes), F32), jax.ShapeDtypeStruct((n, n, lanes), F32)],
        scratch_shapes=[pltpu.VMEM((n, n, lb), F32), pltpu.VMEM((n, lb), F32),
                        pltpu.VMEM((n, lb), F32), pltpu.VMEM((n, lb), F32)],
        compiler_params=_params(("parallel", "arbitrary"), VMEM_LIMIT),
        name="rwkv_scan",
    )(r, w, k, v, kk, a, rk, lng, lnb, s0)


def _merge_kernel(gla_ref, z_ref, g_ref, sb_ref, o_ref):
    o_ref[:, 0:W_GLA] = gla_ref[...]
    o_ref[:, W_GLA:W_GLA + W_RWKV] = (z_ref[...] * g_ref[...]).astype(o_ref.dtype)
    o_ref[:, W_GLA + W_RWKV:D_MODEL] = sb_ref[...]


def _merge(o_gla, z, g, o_sb):
    m = z.shape[0]
    tm = _tile(m, (512, 256, 128, 64, 32, 16))
    row = lambda i: (i, 0)
    return pl.pallas_call(
        _merge_kernel,
        grid=(m // tm,),
        in_specs=[pl.BlockSpec((tm, W_GLA), row), pl.BlockSpec((tm, W_RWKV), row),
                  pl.BlockSpec((tm, W_RWKV), row), pl.BlockSpec((tm, W_SB), row)],
        out_specs=pl.BlockSpec((tm, D_MODEL), row),
        out_shape=jax.ShapeDtypeStruct((m, D_MODEL), BF16),
        compiler_params=_params(("parallel",), VMEM_LIMIT),
        name="merge",
    )(o_gla, z, g, o_sb)


def _pad_lanes(x, lanes):
    return jnp.pad(x, [(0, 0)] * (x.ndim - 1) + [(0, lanes - x.shape[-1])])


def _to_scan_layout(x, batch, seq, lanes):
    x = x.reshape(batch, seq, RWKV_HEADS, RWKV_HEAD).transpose(1, 3, 0, 2)
    return _pad_lanes(x.reshape(seq, RWKV_HEAD, batch * RWKV_HEADS), lanes)


def _from_scan_layout(z, batch, seq):
    z = z[:, :, :batch * RWKV_HEADS].reshape(seq, RWKV_HEAD, batch, RWKV_HEADS)
    return z.transpose(2, 0, 3, 1).reshape(batch * seq, W_RWKV)


def _head_vec_to_scan_layout(x, batch, lanes):
    x = x.reshape(RWKV_HEADS, RWKV_HEAD).T
    return _pad_lanes(jnp.tile(x[:, None, :], (1, batch, 1)).reshape(RWKV_HEAD, batch * RWKV_HEADS), lanes)


def _rwkv_group(seqs, batch, seq, rk, lng, lnb, s0):
    lanes = -(-batch * RWKV_HEADS // LANE) * LANE
    if s0 is None:
        s0_l = jnp.zeros((RWKV_HEAD, RWKV_HEAD, lanes), F32)
    else:
        s0_l = _pad_lanes(s0.transpose(2, 3, 0, 1).reshape(RWKV_HEAD, RWKV_HEAD, batch * RWKV_HEADS), lanes)
    ins = [_to_scan_layout(x, batch, seq, lanes) for x in seqs]
    vecs = [_head_vec_to_scan_layout(x, batch, lanes) for x in (rk, lng, lnb)]
    z, s_t = _rwkv_scan(*ins, *vecs, s0_l)
    s_new = s_t[:, :, :batch * RWKV_HEADS].reshape(RWKV_HEAD, RWKV_HEAD, batch, RWKV_HEADS).transpose(2, 3, 0, 1)
    return _from_scan_layout(z, batch, seq), s_new


def _rearrange_w_in(w):
    d = w.shape[0]
    z = lambda n: jnp.zeros((d, n), w.dtype)
    gla, rwkv, sb = w[:, :P_GLA], w[:, P_GLA:P_GLA + P_RWKV], w[:, P_GLA + P_RWKV:]
    parts = [rwkv, z(P_RWKV_PAD - P_RWKV), sb,
             gla[:, 2 * GLA_QK:2 * GLA_QK + W_GLA], gla[:, 2 * GLA_QK + W_GLA:2 * GLA_QK + 2 * W_GLA],
             gla[:, :GLA_QK], gla[:, GLA_QK:2 * GLA_QK], gla[:, 2 * GLA_QK + 2 * W_GLA:], z(LANE - GLA_LOWRANK)]
    out = jnp.concatenate(parts, axis=1)
    return jnp.concatenate([out, z(P_PAD - out.shape[1])], axis=1).astype(BF16)


def kernel(x_prompt, x_sample, state_gla, state_rwkv, state_rwkv_shift, cache_sb_k, cache_sb_v, norm1_g, w_in, gla_wa2, gla_ba, gla_norm_g, rwkv_mu, rwkv_w0, rwkv_w2, rwkv_a0, rwkv_a2, rwkv_g2, rwkv_k_k, rwkv_k_a, rwkv_r_k, rwkv_ln_g, rwkv_ln_b, sb_norm_g, w_out, norm2_g, w_gate, w_up, w_down, final_g):
    bp, tp, d = x_prompt.shape
    bs, ts, _ = x_sample.shape
    depth = w_in.shape[0]
    mp, ms = bp * tp, bs * ts
    x = jnp.concatenate([x_prompt.reshape(mp, d), x_sample.reshape(ms, d)], axis=0)

    outs = {name: [] for name in ("gla_p", "gla_s", "rw_p", "rw_s", "sh_p", "sh_s", "k_p", "k_s", "v_p", "v_s")}
    row1 = lambda a: a.reshape(1, -1)
    for l in range(depth):
        h = _rmsnorm(x, norm1_g[l], BF16)
        proj = _matmul(h, _rearrange_w_in(w_in[l]), F32)
        proj_p = proj[:mp].reshape(bp, tp, P_PAD)
        proj_s = proj[mp:].reshape(bs, ts, P_PAD)
        outs["sh_p"].append(proj_p[:, -1:, :P_RWKV])
        outs["sh_s"].append(proj_s[:, -1:, :P_RWKV])
        for nm, c0 in (("k", COL_SB_K), ("v", COL_SB_V)):
            outs[nm + "_p"].append(proj_p[:, :, c0:c0 + W_SB].reshape(bp, tp, SB_HEADS, SB_HEAD))
            outs[nm + "_s"].append(proj_s[:, :, c0:c0 + W_SB].reshape(bs, ts, SB_HEADS, SB_HEAD))

        wa_pad = jnp.pad(gla_wa2[l], ((0, LANE - GLA_LOWRANK), (0, 0))).astype(BF16)
        gla_args = (wa_pad, row1(gla_ba[l]), row1(gla_norm_g[l]))
        o_gla_p, sg_p = _gla(proj, 0, bp, tp, *gla_args, jnp.zeros((bp, GLA_HEADS, GLA_DV, GLA_DK), F32))
        o_gla_s, sg_s = _gla(proj, mp, bs, ts, *gla_args, state_gla[l].swapaxes(-1, -2))
        outs["gla_p"].append(sg_p.swapaxes(-1, -2))
        outs["gla_s"].append(sg_s.swapaxes(-1, -2))

        shift_s = jnp.pad(state_rwkv_shift[l], ((0, 0), (0, 0), (0, P_RWKV_PAD - P_RWKV)))
        prev = jnp.concatenate([
            jnp.concatenate([jnp.zeros((bp, 1, P_RWKV_PAD), F32), proj_p[:, :-1, :P_RWKV_PAD]], axis=1).reshape(mp, -1),
            jnp.concatenate([shift_s, proj_s[:, :-1, :P_RWKV_PAD]], axis=1).reshape(ms, -1)], axis=0)
        g2_pad = jnp.pad(rwkv_g2[l], ((0, D_GATE_PAD - D_GATE_LORA), (0, 0))).astype(BF16)
        mu_pad = jnp.pad(rwkv_mu[l], (0, P_RWKV_PAD - P_RWKV))
        *seqs, gate = _rwkv_prep(proj, prev, row1(mu_pad), row1(rwkv_w0[l]), row1(rwkv_a0[l]), row1(rwkv_k_k[l]),
                                 row1(rwkv_k_a[l]), rwkv_w2[l].astype(BF16), rwkv_a2[l].astype(BF16), g2_pad)
        vec_args = (rwkv_r_k[l].reshape(-1), rwkv_ln_g[l], rwkv_ln_b[l])
        z_p, sr_p = _rwkv_group([s[:mp] for s in seqs], bp, tp, *vec_args, None)
        z_s, sr_s = _rwkv_group([s[mp:] for s in seqs], bs, ts, *vec_args, state_rwkv[l])
        outs["rw_p"].append(sr_p)
        outs["rw_s"].append(sr_s)

        sbg = row1(sb_norm_g[l])
        o_sb_p = _sb_prompt(proj, bp, tp, sbg)
        o_sb_s = _sb_sample(proj, mp, bs, ts, cache_sb_k[l], cache_sb_v[l], sbg)

        merged = _merge(jnp.concatenate([o_gla_p, o_gla_s], axis=0), jnp.concatenate([z_p, z_s], axis=0), gate,
                        jnp.concatenate([o_sb_p, o_sb_s], axis=0))
        x = _matmul_residual(merged, w_out[l].astype(BF16), x, tk=d // 2)

        h2 = _rmsnorm(x, norm2_g[l], BF16)
        ff_pad = ((0, 0), (0, D_FF_PAD - D_FF))
        act = _swiglu(h2, jnp.pad(w_gate[l], ff_pad).astype(BF16), jnp.pad(w_up[l], ff_pad).astype(BF16))
        w_dn = jnp.pad(w_down[l], ((0, D_FF_PAD - D_FF), (0, 0))).astype(BF16)
        x = _matmul_residual(act, w_dn, x, tk=D_FF_PAD // 4)

    y = _rmsnorm(x, final_g, F32)
    sd = state_gla.dtype
    st = lambda name, dt=None: jnp.stack(outs[name]) if dt is None else jnp.stack(outs[name]).astype(dt)
    return (y[:mp].reshape(bp, tp, d), y[mp:].reshape(bs, ts, d),
            st("gla_p", sd), st("gla_s", sd), st("rw_p", sd), st("rw_s", sd),
            st("sh_p"), st("sh_s"), st("k_p"), st("k_s"), st("v_p"), st("v_s"))
```

```python
import functools
import math

import jax
import jax.numpy as jnp
from jax import lax
from jax.experimental import pallas as pl
from jax.experimental.pallas import tpu as pltpu

F32 = jnp.float32
BF16 = jnp.bfloat16

D_MODEL = 4096
NORM_EPS = 1e-5
W_GLA = 1536
W_RWKV = 1536
W_SB = 1024
GLA_HEADS = 6
GLA_DV = 256
GLA_DK = 128
GLA_QK = 768
GLA_LOWRANK = 16
GLA_TAU = 16.0
GLA_CHUNK = 64
RWKV_HEAD = 64
RWKV_HEADS = 24
D_LORA = 128
D_GATE_LORA = 480
D_GATE_PAD = 512
RWKV_GN_EPS = 64e-5
SB_HEAD = 128
SB_HEADS = 8
SB_SCALE = SB_HEAD ** -0.5
SB_BLOCK = 128
SB_HEADS_PER_STEP = 4
D_FF = 11008
D_FF_PAD = 11264
P_GLA = 4624
P_RWKV = 5344
P_RWKV_PAD = 5376
P_SB = 3072

COL_RWKV = 0
COL_GLA_Q = P_RWKV_PAD
COL_SB_Q = COL_GLA_Q + GLA_QK
COL_SB_K = COL_SB_Q + W_SB
COL_SB_V = COL_SB_K + W_SB
COL_GLA_V = COL_SB_V + W_SB
COL_GLA_G = COL_GLA_V + W_GLA
COL_GLA_K = COL_GLA_G + W_GLA
COL_GLA_A = COL_GLA_K + GLA_QK
P_PAD = 13312

LANE = 128
SUBLANE = 8
VMEM_LIMIT = 56 * 1024 * 1024

EXP_ZERO_BELOW = -104.0


def _tile(n, candidates):
    for c in candidates:
        if n % c == 0:
            return c
    return n


def _params(sem, vmem=None):
    return pltpu.CompilerParams(dimension_semantics=sem, vmem_limit_bytes=vmem)


def _rmsnorm_kernel(x_ref, g_ref, o_ref):
    x = x_ref[...]
    ms = jnp.mean(x * x, axis=-1, keepdims=True)
    o_ref[...] = (x * lax.rsqrt(ms + NORM_EPS) * g_ref[...]).astype(o_ref.dtype)


def _rmsnorm(x, g, out_dtype):
    m, d = x.shape
    tm = _tile(m, (512, 256, 128, 64, 32, 16, 8))
    return pl.pallas_call(
        _rmsnorm_kernel,
        grid=(m // tm,),
        in_specs=[pl.BlockSpec((tm, d), lambda i: (i, 0)), pl.BlockSpec((1, d), lambda i: (0, 0))],
        out_specs=pl.BlockSpec((tm, d), lambda i: (i, 0)),
        out_shape=jax.ShapeDtypeStruct((m, d), out_dtype),
        compiler_params=_params(("parallel",), VMEM_LIMIT),
        name="rmsnorm",
    )(x, g.reshape(1, d))


def _mm_kernel(x_ref, w_ref, o_ref):
    o_ref[...] = jnp.dot(x_ref[...], w_ref[...], preferred_element_type=F32).astype(o_ref.dtype)


def _mm_res_kernel(x_ref, w_ref, r_ref, o_ref):
    acc = jnp.dot(x_ref[...], w_ref[...], preferred_element_type=F32)
    k = pl.program_id(2)

    @pl.when(k == 0)
    def _():
        o_ref[...] = r_ref[...] + acc

    @pl.when(k != 0)
    def _():
        o_ref[...] += acc


def _swiglu_kernel(x_ref, wg_ref, wu_ref, o_ref):
    x = x_ref[...]
    g = jnp.dot(x, wg_ref[...], preferred_element_type=F32)
    u = jnp.dot(x, wu_ref[...], preferred_element_type=F32)
    o_ref[...] = (g * jax.nn.sigmoid(g) * u).astype(o_ref.dtype)


def _matmul(x, w, out_dtype):
    m, k = x.shape
    n = w.shape[1]
    tm = _tile(m, (1024, 512, 256, 128, 64, 32, 16, 8))
    tn = _tile(n, (1024, 512, 256, 128))
    return pl.pallas_call(
        _mm_kernel,
        grid=(m // tm, n // tn),
        in_specs=[pl.BlockSpec((tm, k), lambda i, j: (i, 0)), pl.BlockSpec((k, tn), lambda i, j: (0, j))],
        out_specs=pl.BlockSpec((tm, tn), lambda i, j: (i, j)),
        out_shape=jax.ShapeDtypeStruct((m, n), out_dtype),
        compiler_params=_params(("parallel", "parallel"), VMEM_LIMIT),
        name="matmul",
    )(x, w)


def _matmul_residual(x, w, res, tk):
    m, k = x.shape
    n = w.shape[1]
    tm = _tile(m, (1024, 512, 256, 128, 64, 32, 16, 8))
    tn = _tile(n, (1024, 512, 256, 128))
    return pl.pallas_call(
        _mm_res_kernel,
        grid=(m // tm, n // tn, k // tk),
        in_specs=[pl.BlockSpec((tm, tk), lambda i, j, kk: (i, kk)),
                  pl.BlockSpec((tk, tn), lambda i, j, kk: (kk, j)),
                  pl.BlockSpec((tm, tn), lambda i, j, kk: (i, j))],
        out_specs=pl.BlockSpec((tm, tn), lambda i, j, kk: (i, j)),
        out_shape=jax.ShapeDtypeStruct((m, n), F32),
        compiler_params=_params(("parallel", "parallel", "arbitrary"), VMEM_LIMIT),
        name="matmul_residual",
    )(x, w, res)


def _swiglu(x, wg, wu):
    m, k = x.shape
    n = wg.shape[1]
    tm = _tile(m, (1024, 512, 256, 128, 64, 32, 16, 8))
    tn = _tile(n, (512, 256, 128))
    return pl.pallas_call(
        _swiglu_kernel,
        grid=(m // tm, n // tn),
        in_specs=[pl.BlockSpec((tm, k), lambda i, j: (i, 0)),
                  pl.BlockSpec((k, tn), lambda i, j: (0, j)),
                  pl.BlockSpec((k, tn), lambda i, j: (0, j))],
        out_specs=pl.BlockSpec((tm, tn), lambda i, j: (i, j)),
        out_shape=jax.ShapeDtypeStruct((m, n), BF16),
        compiler_params=_params(("parallel", "parallel"), VMEM_LIMIT),
        name="swiglu",
    )(x, wg, wu)


def _log_sigmoid(x):
    return jnp.minimum(x, 0.0) - jnp.log1p(jnp.exp(-jnp.abs(x)))


def _split_bf16(x):
    hi = x.astype(BF16)
    lo = (x - hi.astype(F32)).astype(BF16)
    return hi, lo


def _dot_nt(a, b):
    return lax.dot_general(a, b, (((1,), (1,)), ((), ())), preferred_element_type=F32)


def _dot_tn(a, b):
    return lax.dot_general(a, b, (((0,), (0,)), ((), ())), preferred_element_type=F32)


def _gla_kernel(q_ref, k_ref, v_ref, g_ref, al_ref, wa_ref, ba_ref, ng_ref, s0_ref, o_ref, sT_ref, s_scr,
                *, chunk, n_chunks, n_steps):
    c = pl.program_id(2)

    @pl.when(c == 0)
    def _():
        s_scr[...] = s0_ref[...]

    row = lax.broadcasted_iota(jnp.int32, (chunk, chunk), 0)
    col = lax.broadcasted_iota(jnp.int32, (chunk, chunk), 1)
    lower = col <= row
    tri = jnp.where(lower, 1.0, 0.0).astype(BF16)
    wa = wa_ref[...]
    ba = ba_ref[...]
    ng = ng_ref[...]

    for ci in range(n_chunks):
        rows = pl.ds(ci * chunk, chunk)
        x = jnp.dot(al_ref[rows, :].astype(BF16), wa, preferred_element_type=F32) + ba
        log_a = _log_sigmoid(x) * (1.0 / GLA_TAU)
        hi, lo = _split_bf16(log_a)
        b = jnp.dot(tri, hi, preferred_element_type=F32) + jnp.dot(tri, lo, preferred_element_type=F32)
        b_last = b[chunk - 1:chunk, :]
        q = q_ref[rows, :] * (GLA_DK ** -0.5)
        k = k_ref[rows, :]
        v = v_ref[rows, :].astype(BF16)
        qd = (q * jnp.exp(b)).astype(BF16)
        kd = (k * jnp.exp(-b)).astype(BF16)
        scores = jnp.where(lower, _dot_nt(qd, kd), 0.0).astype(BF16)
        s_t = s_scr[...]
        o = jnp.dot(scores, v, preferred_element_type=F32) + _dot_nt(qd, s_t.astype(BF16))
        kdec = (k * jnp.exp(b_last - b)).astype(BF16)
        s_scr[...] = s_t * jnp.exp(b_last) + _dot_tn(v, kdec)
        ms = jnp.mean(o * o, axis=-1, keepdims=True)
        gg = g_ref[rows, :]
        o_ref[rows, :] = (o * lax.rsqrt(ms + NORM_EPS) * ng * (gg * jax.nn.sigmoid(gg))).astype(o_ref.dtype)

    @pl.when(c == n_steps - 1)
    def _():
        sT_ref[...] = s_scr[...]


def _gla(proj, row0, batch, seq, wa_pad, ba, norm_g, s0_t):
    chunk = min(GLA_CHUNK, seq)
    tb = _tile(seq, (256, 128, 64, 32, 16, 8))
    n_chunks = tb // chunk
    n_steps = seq // tb
    rb0 = row0 // tb

    def rows(b, c):
        return rb0 + b * n_steps + c

    kern = functools.partial(_gla_kernel, chunk=chunk, n_chunks=n_chunks, n_steps=n_steps)
    return pl.pallas_call(
        kern,
        grid=(batch, GLA_HEADS, n_steps),
        in_specs=[
            pl.BlockSpec((tb, GLA_DK), lambda b, h, c: (rows(b, c), COL_GLA_Q // GLA_DK + h)),
            pl.BlockSpec((tb, GLA_DK), lambda b, h, c: (rows(b, c), COL_GLA_K // GLA_DK + h)),
            pl.BlockSpec((tb, GLA_DV), lambda b, h, c: (rows(b, c), COL_GLA_V // GLA_DV + h)),
            pl.BlockSpec((tb, GLA_DV), lambda b, h, c: (rows(b, c), COL_GLA_G // GLA_DV + h)),
            pl.BlockSpec((tb, LANE), lambda b, h, c: (rows(b, c), COL_GLA_A // LANE)),
            pl.BlockSpec((LANE, GLA_DK), lambda b, h, c: (0, h)),
            pl.BlockSpec((1, GLA_DK), lambda b, h, c: (0, h)),
            pl.BlockSpec((1, GLA_DV), lambda b, h, c: (0, 0)),
            pl.BlockSpec((None, None, GLA_DV, GLA_DK), lambda b, h, c: (b, h, 0, 0)),
        ],
        out_specs=[
            pl.BlockSpec((tb, GLA_DV), lambda b, h, c: (b * n_steps + c, h)),
            pl.BlockSpec((None, None, GLA_DV, GLA_DK), lambda b, h, c: (b, h, 0, 0)),
        ],
        out_shape=[
            jax.ShapeDtypeStruct((batch * seq, W_GLA), BF16),
            jax.ShapeDtypeStruct((batch, GLA_HEADS, GLA_DV, GLA_DK), F32),
        ],
        scratch_shapes=[pltpu.VMEM((GLA_DV, GLA_DK), F32)],
        compiler_params=_params(("parallel", "parallel", "arbitrary"), VMEM_LIMIT),
        name="gla",
    )(proj, proj, proj, proj, proj, wa_pad, ba, norm_g, s0_t)


def _sb_suffix_and_total():
    r2 = lax.broadcasted_iota(jnp.int32, (SB_BLOCK, 2 * SB_BLOCK), 0)
    c2 = lax.broadcasted_iota(jnp.int32, (SB_BLOCK, 2 * SB_BLOCK), 1)
    return jnp.where((r2 > c2) | (c2 >= SB_BLOCK), 1.0, 0.0).astype(BF16)


def _sb_causal(n_rows):
    row = lax.broadcasted_iota(jnp.int32, (n_rows, SB_BLOCK), 0)
    col = lax.broadcasted_iota(jnp.int32, (n_rows, SB_BLOCK), 1)
    return col < row


def _head_lanes(j):
    return slice(j * SB_HEAD, (j + 1) * SB_HEAD)


def _sb_block(q, kblk, vblk, run, acc, suffix_and_total, causal):
    z = _dot_nt(q, kblk) * SB_SCALE
    sp = jnp.maximum(z, 0.0) + jnp.log1p(jnp.exp(-jnp.abs(z)))
    log_1m = -sp
    if causal is not None:
        log_1m = jnp.where(causal, log_1m, 0.0)
    log_beta = z - sp
    hi, lo = _split_bf16(log_1m)
    cs = (jnp.dot(hi, suffix_and_total, preferred_element_type=F32)
          + jnp.dot(lo, suffix_and_total, preferred_element_type=F32))
    att = jnp.exp(log_beta + cs[:, :SB_BLOCK] + run)
    if causal is not None:
        att = jnp.where(causal, att, 0.0)
    acc = acc + jnp.dot(att.astype(BF16), vblk, preferred_element_type=F32)
    return run + cs[:, SB_BLOCK:], acc


def _sb_sweep(qs, k_ref, v_ref, n_blocks, runs, accs, suffix_and_total):
    nh = len(qs)

    def run_max(runs):
        return functools.reduce(jnp.maximum, [jnp.max(r) for r in runs])

    def cond(carry):
        i, _, _, rmax = carry
        return jnp.logical_and(i < n_blocks, rmax > EXP_ZERO_BELOW)

    def body(carry):
        i, runs, accs, _ = carry
        rows = pl.ds(pl.multiple_of((n_blocks - 1 - i) * SB_BLOCK, SB_BLOCK), SB_BLOCK)
        new = [_sb_block(qs[j], k_ref[rows, _head_lanes(j)].astype(BF16), v_ref[rows, _head_lanes(j)].astype(BF16),
                         runs[j], accs[j], suffix_and_total, None) for j in range(nh)]
        runs = tuple(n[0] for n in new)
        return i + 1, runs, tuple(n[1] for n in new), run_max(runs)

    _, _, accs, _ = lax.while_loop(cond, body, (jnp.int32(0), tuple(runs), tuple(accs), run_max(runs)))
    return accs


def _sb_finish(accs, g_ref, o_ref):
    for j, acc in enumerate(accs):
        ms = jnp.mean(acc * acc, axis=-1, keepdims=True)
        o_ref[:, _head_lanes(j)] = (acc * lax.rsqrt(ms + NORM_EPS) * g_ref[:, _head_lanes(j)]).astype(o_ref.dtype)


def _sb_prompt_kernel(q_ref, k_ref, v_ref, g_ref, o_ref):
    qi = pl.program_id(2)
    suffix_and_total = _sb_suffix_and_total()
    causal = _sb_causal(SB_BLOCK)
    diag = pl.ds(pl.multiple_of(qi * SB_BLOCK, SB_BLOCK), SB_BLOCK)
    zero = jnp.zeros((SB_BLOCK, SB_BLOCK), F32)
    qs, runs, accs = [], [], []
    for j in range(SB_HEADS_PER_STEP):
        q = q_ref[:, _head_lanes(j)].astype(BF16)
        run, acc = _sb_block(q, k_ref[diag, _head_lanes(j)].astype(BF16), v_ref[diag, _head_lanes(j)].astype(BF16),
                             zero, zero, suffix_and_total, causal)
        qs.append(q), runs.append(run), accs.append(acc)
    _sb_finish(_sb_sweep(qs, k_ref, v_ref, qi, runs, accs, suffix_and_total), g_ref, o_ref)


def _sb_prompt(proj, batch, seq, norm_g):
    nq = seq // SB_BLOCK
    wide = SB_HEADS_PER_STEP * SB_HEAD
    return pl.pallas_call(
        _sb_prompt_kernel,
        grid=(batch, SB_HEADS // SB_HEADS_PER_STEP, nq),
        in_specs=[
            pl.BlockSpec((SB_BLOCK, wide), lambda b, h, i: (b * nq + i, COL_SB_Q // wide + h)),
            pl.BlockSpec((seq, wide), lambda b, h, i: (b, COL_SB_K // wide + h)),
            pl.BlockSpec((seq, wide), lambda b, h, i: (b, COL_SB_V // wide + h)),
            pl.BlockSpec((1, wide), lambda b, h, i: (0, h)),
        ],
        out_specs=pl.BlockSpec((SB_BLOCK, wide), lambda b, h, i: (b * nq + i, h)),
        out_shape=jax.ShapeDtypeStruct((batch * seq, W_SB), BF16),
        compiler_params=_params(("parallel", "parallel", "arbitrary"), VMEM_LIMIT),
        name="sb_prompt",
    )(proj, proj, proj, norm_g)


def _sb_sample_kernel(q_ref, k_ref, v_ref, kp_ref, vp_ref, g_ref, o_ref, *, seq, n_past_blocks):
    suffix_and_total = _sb_suffix_and_total()
    causal = _sb_causal(seq)
    pad = jnp.zeros((SB_BLOCK - seq, SB_HEAD), BF16)
    zero = jnp.zeros((seq, SB_BLOCK), F32)
    qs, runs, accs = [], [], []
    for j in range(SB_HEADS_PER_STEP):
        q = q_ref[:, _head_lanes(j)].astype(BF16)
        kblk = jnp.concatenate([k_ref[:, _head_lanes(j)].astype(BF16), pad], axis=0)
        vblk = jnp.concatenate([v_ref[:, _head_lanes(j)].astype(BF16), pad], axis=0)
        run, acc = _sb_block(q, kblk, vblk, zero, zero, suffix_and_total, causal)
        qs.append(q), runs.append(run), accs.append(acc)
    _sb_finish(_sb_sweep(qs, kp_ref, vp_ref, n_past_blocks, runs, accs, suffix_and_total), g_ref, o_ref)


def _sb_sample(proj, row0, batch, seq, k_past, v_past, norm_g):
    past = k_past.shape[1]
    rb0 = row0 // seq
    wide = SB_HEADS_PER_STEP * SB_HEAD
    kern = functools.partial(_sb_sample_kernel, seq=seq, n_past_blocks=past // SB_BLOCK)
    return pl.pallas_call(
        kern,
        grid=(batch, SB_HEADS // SB_HEADS_PER_STEP),
        in_specs=[
            pl.BlockSpec((seq, wide), lambda b, h: (rb0 + b, COL_SB_Q // wide + h)),
            pl.BlockSpec((seq, wide), lambda b, h: (rb0 + b, COL_SB_K // wide + h)),
            pl.BlockSpec((seq, wide), lambda b, h: (rb0 + b, COL_SB_V // wide + h)),
            pl.BlockSpec((None, past, wide), lambda b, h: (b, 0, h)),
            pl.BlockSpec((None, past, wide), lambda b, h: (b, 0, h)),
            pl.BlockSpec((1, wide), lambda b, h: (0, h)),
        ],
        out_specs=pl.BlockSpec((seq, wide), lambda b, h: (b, h)),
        out_shape=jax.ShapeDtypeStruct((batch * seq, W_SB), BF16),
        compiler_params=_params(("parallel", "parallel"), VMEM_LIMIT),
        name="sb_sample",
    )(proj, proj, proj, k_past.reshape(batch, past, W_SB), v_past.reshape(batch, past, W_SB), norm_g)


def _rwkv_prep_kernel(p_ref, bnd_ref, mu_ref, w0_ref, a0_ref, kk_ref, ka_ref, w2_ref, a2_ref, g2_ref,
                      r_out, w_out, k_out, v_out, kk_out, a_out, g_out):
    p = p_ref[...]
    first = lax.broadcasted_iota(jnp.int32, p.shape, 0) == 0
    prev = jnp.where(first, bnd_ref[...], pltpu.roll(p, 1, 0))
    pm = p + (prev - p) * mu_ref[...]
    o = 0
    r = pm[:, o:o + W_RWKV]; o += W_RWKV
    w_low = pm[:, o:o + D_LORA]; o += D_LORA
    k = pm[:, o:o + W_RWKV]; o += W_RWKV
    v = pm[:, o:o + W_RWKV]; o += W_RWKV
    a_low = pm[:, o:o + D_LORA]; o += D_LORA
    g_low = pm[:, o:o + D_GATE_PAD]
    log_w = -math.exp(-0.5) * jax.nn.sigmoid(
        w0_ref[...] + jnp.dot(jnp.tanh(w_low).astype(BF16), w2_ref[...], preferred_element_type=F32))
    a = jax.nn.sigmoid(a0_ref[...] + jnp.dot(a_low.astype(BF16), a2_ref[...], preferred_element_type=F32))
    g = jnp.dot(jax.nn.sigmoid(g_low).astype(BF16), g2_ref[...], preferred_element_type=F32)
    r_out[...] = r
    w_out[...] = jnp.exp(log_w)
    k_out[...] = k * (1.0 + (a - 1.0) * ka_ref[...])
    v_out[...] = v
    kk_out[...] = k * kk_ref[...]
    a_out[...] = a
    g_out[...] = g


def _rwkv_prep(proj, row0, batch, seq, shift, mu, w0, a0, k_k, k_a, w2, a2, g2):
    tm = _tile(seq, (128, 64, 32, 16, 8))
    nt = seq // tm
    m = batch * seq
    rb0 = row0 // tm
    last = proj[row0 + tm - 1:row0 + m:tm, :P_RWKV_PAD].reshape(batch, nt, P_RWKV_PAD)
    bnd = jnp.concatenate([shift[:, None, :], last[:, :nt - 1]], axis=1).reshape(batch * nt, 1, P_RWKV_PAD)
    row = lambda i: (i, 0)
    fixed = lambda i: (0, 0)
    vec = pl.BlockSpec((1, W_RWKV), fixed)
    out = pl.BlockSpec((tm, W_RWKV), row)
    return pl.pallas_call(
        _rwkv_prep_kernel,
        grid=(m // tm,),
        in_specs=[pl.BlockSpec((tm, P_RWKV_PAD), lambda i: (rb0 + i, 0)),
                  pl.BlockSpec((None, 1, P_RWKV_PAD), lambda i: (i, 0, 0)),
                  pl.BlockSpec((1, P_RWKV_PAD), fixed), vec, vec, vec, vec,
                  pl.BlockSpec((D_LORA, W_RWKV), fixed), pl.BlockSpec((D_LORA, W_RWKV), fixed),
                  pl.BlockSpec((D_GATE_PAD, W_RWKV), fixed)],
        out_specs=[out] * 7,
        out_shape=[jax.ShapeDtypeStruct((m, W_RWKV), F32)] * 7,
        compiler_params=_params(("parallel",), VMEM_LIMIT),
        name="rwkv_prep",
    )(proj, bnd, mu, w0, a0, k_k, k_a, w2, a2, g2)


def _rwkv_scan_kernel(r_ref, w_ref, k_ref, v_ref, kk_ref, a_ref, rk_ref, lng_ref, lnb_ref, s0_ref,
                      z_ref, sT_ref, s_scr, kk_scr, b_scr, w_scr, k_scr, r_scr, v_scr, y_scr,
                      *, t_block, n_steps, vs):
    c = pl.program_id(1)
    n = RWKV_HEAD
    nv = n // vs
    lanes_in = r_ref.shape[-1]
    lanes_work = lanes_in * vs

    def dup(x):
        return x if vs == 1 else jnp.concatenate([x] * vs, axis=1)

    @pl.when(c == 0)
    def _():
        s_scr[...] = s0_ref[...]

    def step(t, carry):
        kk_raw = kk_ref[t]
        norm = jnp.sqrt(jnp.sum(kk_raw * kk_raw, axis=0, keepdims=True))
        kk = kk_raw / jnp.maximum(norm, 1e-12)
        kk_scr[...] = dup(kk)
        b_scr[...] = dup(kk * a_ref[t])
        w_scr[...] = dup(w_ref[t])
        k_scr[...] = dup(k_ref[t])
        r_scr[...] = dup(r_ref[t])
        v_t = v_ref[t]
        v_scr[...] = v_t if vs == 1 else jnp.concatenate([v_t[g * nv:(g + 1) * nv] for g in range(vs)], axis=1)

        for lg in range(lanes_work // LANE):
            ls = pl.ds(lg * LANE, LANE)

            def value_rows(vo, carry2):
                v0 = pl.multiple_of(vo * SUBLANE, SUBLANE)
                v_tile = v_scr[pl.ds(v0, SUBLANE), ls]
                y_rows = []
                for vi in range(SUBLANE):
                    s_v = s_scr[v0 + vi, :, ls]
                    sa = jnp.sum(s_v * kk_scr[:, ls], axis=0, keepdims=True)
                    s_new = s_v * w_scr[:, ls] - sa * b_scr[:, ls] + v_tile[vi:vi + 1, :] * k_scr[:, ls]
                    s_scr[v0 + vi, :, ls] = s_new
                    y_rows.append(jnp.sum(s_new * r_scr[:, ls], axis=0, keepdims=True))
                y_scr[pl.ds(v0, SUBLANE), ls] = jnp.concatenate(y_rows, axis=0)
                return carry2

            lax.fori_loop(0, nv // SUBLANE, value_rows, 0)

        y2 = y_scr[...]
        y = y2 if vs == 1 else jnp.concatenate([y2[:, g * lanes_in:(g + 1) * lanes_in] for g in range(vs)], axis=0)
        mean = jnp.mean(y, axis=0, keepdims=True)
        d = y - mean
        var = jnp.mean(d * d, axis=0, keepdims=True)
        bonus = jnp.sum(r_ref[t] * k_ref[t] * rk_ref[...], axis=0, keepdims=True) * v_t
        z_ref[t] = d * lax.rsqrt(var + RWKV_GN_EPS) * lng_ref[...] + lnb_ref[...] + bonus
        return carry

    lax.fori_loop(0, t_block, step, 0)

    @pl.when(c == n_steps - 1)
    def _():
        sT_ref[...] = s_scr[...]


def _rwkv_scan(r, w, k, v, kk, a, rk, lng, lnb, s0, vs):
    seq, n, lanes = r.shape
    nv = n // vs
    t_block = _tile(seq, (8, 4, 2, 1))
    n_steps = seq // t_block
    lb = lanes if vs > 1 else _tile(lanes, (3 * LANE, 2 * LANE, LANE))
    lbw = lb * vs
    seq_spec = pl.BlockSpec((t_block, n, lb), lambda g, c: (c, 0, g))
    vec_spec = pl.BlockSpec((n, lb), lambda g, c: (0, g))
    st_spec = pl.BlockSpec((nv, n, lbw), lambda g, c: (0, 0, g))
    kern = functools.partial(_rwkv_scan_kernel, t_block=t_block, n_steps=n_steps, vs=vs)
    work = lambda rows: pltpu.VMEM((rows, lbw), F32)
    return pl.pallas_call(
        kern,
        grid=(lanes // lb, n_steps),
        in_specs=[seq_spec] * 6 + [vec_spec] * 3 + [st_spec],
        out_specs=[seq_spec, st_spec],
        out_shape=[jax.ShapeDtypeStruct((seq, n, lanes), F32), jax.ShapeDtypeStruct((nv, n, lanes * vs), F32)],
        scratch_shapes=[pltpu.VMEM((nv, n, lbw), F32), work(n), work(n), work(n), work(n), work(n),
                        work(nv), work(nv)],
        compiler_params=_params(("parallel", "arbitrary"), VMEM_LIMIT),
        name="rwkv_scan",
    )(r, w, k, v, kk, a, rk, lng, lnb, s0)


def _merge_kernel(gla_p, z_p, g_p, sb_p, gla_s, z_s, g_s, sb_s, o_ref, *, n_prompt_tiles):
    def write(gla_ref, z_ref, g_ref, sb_ref):
        o_ref[:, 0:W_GLA] = gla_ref[...]
        o_ref[:, W_GLA:W_GLA + W_RWKV] = (z_ref[...] * g_ref[...]).astype(o_ref.dtype)
        o_ref[:, W_GLA + W_RWKV:D_MODEL] = sb_ref[...]

    i = pl.program_id(0)

    @pl.when(i < n_prompt_tiles)
    def _():
        write(gla_p, z_p, g_p, sb_p)

    @pl.when(i >= n_prompt_tiles)
    def _():
        write(gla_s, z_s, g_s, sb_s)


def _merge(prompt, sample):
    mp, ms = prompt[1].shape[0], sample[1].shape[0]
    tm = _tile(math.gcd(mp, ms), (512, 256, 128, 64, 32, 16))
    n_p, n_s = mp // tm, ms // tm
    widths = (W_GLA, W_RWKV, W_RWKV, W_SB)
    p_specs = [pl.BlockSpec((tm, wd), lambda i: (jnp.minimum(i, n_p - 1), 0)) for wd in widths]
    s_specs = [pl.BlockSpec((tm, wd), lambda i: (jnp.maximum(i - n_p, 0), 0)) for wd in widths]
    return pl.pallas_call(
        functools.partial(_merge_kernel, n_prompt_tiles=n_p),
        grid=(n_p + n_s,),
        in_specs=p_specs + s_specs,
        out_specs=pl.BlockSpec((tm, D_MODEL), lambda i: (i, 0)),
        out_shape=jax.ShapeDtypeStruct((mp + ms, D_MODEL), BF16),
        compiler_params=_params(("parallel",), VMEM_LIMIT),
        name="merge",
    )(*prompt, *sample)


def _to_scan_layout(x, batch, seq):
    x = x.reshape(batch, seq, RWKV_HEADS, RWKV_HEAD).transpose(1, 3, 0, 2)
    return x.reshape(seq, RWKV_HEAD, batch * RWKV_HEADS)


def _from_scan_layout(z, batch, seq):
    z = z.reshape(seq, RWKV_HEAD, batch, RWKV_HEADS)
    return z.transpose(2, 0, 3, 1).reshape(batch * seq, W_RWKV)


def _head_vec_to_scan_layout(x, batch):
    x = x.reshape(RWKV_HEADS, RWKV_HEAD).T
    return jnp.tile(x[:, None, :], (1, batch, 1)).reshape(RWKV_HEAD, batch * RWKV_HEADS)


def _rwkv_group(seqs, batch, seq, rk, lng, lnb, s0):
    lanes = batch * RWKV_HEADS
    n = RWKV_HEAD
    vs = next(f for f in (1, 2, 4, 8) if (lanes * f) % LANE == 0)
    nv = n // vs
    s0_l = s0.transpose(2, 3, 0, 1).reshape(vs, nv, n, lanes).transpose(1, 2, 0, 3).reshape(nv, n, vs * lanes)
    ins = [_to_scan_layout(x, batch, seq) for x in seqs]
    vecs = [_head_vec_to_scan_layout(x, batch) for x in (rk, lng, lnb)]
    z, s_t = _rwkv_scan(*ins, *vecs, s0_l, vs)
    s_new = s_t.reshape(nv, n, vs, lanes).transpose(2, 0, 1, 3).reshape(n, n, batch, RWKV_HEADS).transpose(2, 3, 0, 1)
    return _from_scan_layout(z, batch, seq), s_new


def _rearrange_w_in(w):
    d = w.shape[0]
    w = w.astype(BF16)
    z = lambda n: jnp.zeros((d, n), BF16)
    gla, rwkv, sb = w[:, :P_GLA], w[:, P_GLA:P_GLA + P_RWKV], w[:, P_GLA + P_RWKV:]
    parts = [rwkv, z(P_RWKV_PAD - P_RWKV), gla[:, :GLA_QK], sb,
             gla[:, 2 * GLA_QK:2 * GLA_QK + W_GLA], gla[:, 2 * GLA_QK + W_GLA:2 * GLA_QK + 2 * W_GLA],
             gla[:, GLA_QK:2 * GLA_QK], gla[:, 2 * GLA_QK + 2 * W_GLA:], z(LANE - GLA_LOWRANK)]
    width = sum(p.shape[1] for p in parts)
    return jnp.concatenate(parts + [z(P_PAD - width)], axis=1)


def kernel(x_prompt, x_sample, state_gla, state_rwkv, state_rwkv_shift, cache_sb_k, cache_sb_v, norm1_g, w_in, gla_wa2, gla_ba, gla_norm_g, rwkv_mu, rwkv_w0, rwkv_w2, rwkv_a0, rwkv_a2, rwkv_g2, rwkv_k_k, rwkv_k_a, rwkv_r_k, rwkv_ln_g, rwkv_ln_b, sb_norm_g, w_out, norm2_g, w_gate, w_up, w_down, final_g):
    bp, tp, d = x_prompt.shape
    bs, ts, _ = x_sample.shape
    depth = w_in.shape[0]
    mp, ms = bp * tp, bs * ts
    x = jnp.concatenate([x_prompt.reshape(mp, d), x_sample.reshape(ms, d)], axis=0)

    outs = {name: [] for name in ("gla_p", "gla_s", "rw_p", "rw_s", "sh_p", "sh_s", "k_p", "k_s", "v_p", "v_s")}
    row1 = lambda a: a.reshape(1, -1)
    for l in range(depth):
        h = _rmsnorm(x, norm1_g[l], BF16)
        proj = _matmul(h, _rearrange_w_in(w_in[l]), F32)
        outs["sh_p"].append(proj[tp - 1:mp:tp, :P_RWKV].reshape(bp, 1, P_RWKV))
        outs["sh_s"].append(proj[mp + ts - 1::ts, :P_RWKV].reshape(bs, 1, P_RWKV))
        for nm, c0 in (("k", COL_SB_K), ("v", COL_SB_V)):
            outs[nm + "_p"].append(proj[:mp, c0:c0 + W_SB].reshape(bp, tp, SB_HEADS, SB_HEAD))
            outs[nm + "_s"].append(proj[mp:, c0:c0 + W_SB].reshape(bs, ts, SB_HEADS, SB_HEAD))

        wa_pad = jnp.pad(gla_wa2[l], ((0, LANE - GLA_LOWRANK), (0, 0))).astype(BF16)
        gla_args = (wa_pad, row1(gla_ba[l]), row1(gla_norm_g[l]))
        o_gla_p, sg_p = _gla(proj, 0, bp, tp, *gla_args, jnp.zeros((bp, GLA_HEADS, GLA_DV, GLA_DK), F32))
        o_gla_s, sg_s = _gla(proj, mp, bs, ts, *gla_args, state_gla[l].swapaxes(-1, -2))
        outs["gla_p"].append(sg_p.swapaxes(-1, -2))
        outs["gla_s"].append(sg_s.swapaxes(-1, -2))

        lane_pad = (0, P_RWKV_PAD - P_RWKV)
        prep_args = (row1(jnp.pad(rwkv_mu[l], lane_pad)), row1(rwkv_w0[l]), row1(rwkv_a0[l]), row1(rwkv_k_k[l]),
                     row1(rwkv_k_a[l]), rwkv_w2[l].astype(BF16), rwkv_a2[l].astype(BF16),
                     jnp.pad(rwkv_g2[l], ((0, D_GATE_PAD - D_GATE_LORA), (0, 0))).astype(BF16))
        vec_args = (rwkv_r_k[l].reshape(-1), rwkv_ln_g[l], rwkv_ln_b[l])
        *seqs_p, gate_p = _rwkv_prep(proj, 0, bp, tp, jnp.zeros((bp, P_RWKV_PAD), F32), *prep_args)
        *seqs_s, gate_s = _rwkv_prep(proj, mp, bs, ts, jnp.pad(state_rwkv_shift[l][:, 0, :], ((0, 0), lane_pad)),
                                     *prep_args)
        z_p, sr_p = _rwkv_group(seqs_p, bp, tp, *vec_args, jnp.zeros((bp, RWKV_HEADS, RWKV_HEAD, RWKV_HEAD), F32))
        z_s, sr_s = _rwkv_group(seqs_s, bs, ts, *vec_args, state_rwkv[l])
        outs["rw_p"].append(sr_p)
        outs["rw_s"].append(sr_s)

        sbg = row1(sb_norm_g[l])
        o_sb_p = _sb_prompt(proj, bp, tp, sbg)
        o_sb_s = _sb_sample(proj, mp, bs, ts, cache_sb_k[l], cache_sb_v[l], sbg)

        merged = _merge((o_gla_p, z_p, gate_p, o_sb_p), (o_gla_s, z_s, gate_s, o_sb_s))
        x = _matmul_residual(merged, w_out[l].astype(BF16), x, tk=d // 2)

        h2 = _rmsnorm(x, norm2_g[l], BF16)
        ff_pad = ((0, 0), (0, D_FF_PAD - D_FF))
        act = _swiglu(h2, jnp.pad(w_gate[l].astype(BF16), ff_pad), jnp.pad(w_up[l].astype(BF16), ff_pad))
        w_dn = jnp.pad(w_down[l].astype(BF16), ((0, D_FF_PAD - D_FF), (0, 0)))
        x = _matmul_residual(act, w_dn, x, tk=D_FF_PAD // 4)

    y = _rmsnorm(x, final_g, F32)
    sd = state_gla.dtype
    st = lambda name, dt=None: jnp.stack(outs[name]) if dt is None else jnp.stack(outs[name]).astype(dt)
    return (y[:mp].reshape(bp, tp, d), y[mp:].reshape(bs, ts, d),
            st("gla_p", sd), st("gla_s", sd), st("rw_p", sd), st("rw_s", sd),
            st("sh_p"), st("sh_s"), st("k_p"), st("k_s"), st("v_p"), st("v_s"))
```

```python
import functools
import math

import jax
import jax.numpy as jnp
import numpy as np
from jax import lax
from jax.experimental import pallas as pl
from jax.experimental.pallas import tpu as pltpu

F32 = jnp.float32
BF16 = jnp.bfloat16

D_MODEL = 4096
NORM_EPS = 1e-5
W_GLA = 1536
W_RWKV = 1536
W_SB = 1024
GLA_HEADS = 6
GLA_DV = 256
GLA_DK = 128
GLA_QK = 768
GLA_LOWRANK = 16
GLA_TAU = 16.0
GLA_CHUNK = 64
GLA_HEADS_PER_STEP = 3
RWKV_HEAD = 64
RWKV_HEADS = 24
D_LORA = 128
D_GATE_LORA = 480
D_GATE_PAD = 512
RWKV_GN_EPS = 64e-5
SB_HEAD = 128
SB_HEADS = 8
SB_SCALE = SB_HEAD ** -0.5
SB_BLOCK = 128
SB_HEADS_PER_STEP = 8
D_FF = 11008
D_FF_PAD = 11264
P_GLA = 4624
P_RWKV = 5344
P_RWKV_PAD = 5376
P_SB = 3072

COL_RWKV = 0
COL_GLA_Q = P_RWKV_PAD
COL_SB_Q = COL_GLA_Q + GLA_QK
COL_SB_K = COL_SB_Q + W_SB
COL_SB_V = COL_SB_K + W_SB
COL_GLA_V = COL_SB_V + W_SB
COL_GLA_G = COL_GLA_V + W_GLA
COL_GLA_K = COL_GLA_G + W_GLA
COL_GLA_A = COL_GLA_K + GLA_QK
P_PAD = 13312

LANE = 128
SUBLANE = 8
VMEM_LIMIT = 56 * 1024 * 1024

EXP_ZERO_BELOW = -104.0


def _tile(n, candidates):
    for c in candidates:
        if n % c == 0:
            return c
    return n


def _params(sem, vmem=None):
    return pltpu.CompilerParams(dimension_semantics=sem, vmem_limit_bytes=vmem)


def _rmsnorm_kernel(x_ref, g_ref, o_ref):
    x = x_ref[...]
    ms = jnp.mean(x * x, axis=-1, keepdims=True)
    o_ref[...] = (x * lax.rsqrt(ms + NORM_EPS) * g_ref[...]).astype(o_ref.dtype)


def _rmsnorm(x, g, out_dtype):
    m, d = x.shape
    tm = _tile(m, (512, 256, 128, 64, 32, 16, 8))
    return pl.pallas_call(
        _rmsnorm_kernel,
        grid=(m // tm,),
        in_specs=[pl.BlockSpec((tm, d), lambda i: (i, 0)), pl.BlockSpec((1, d), lambda i: (0, 0))],
        out_specs=pl.BlockSpec((tm, d), lambda i: (i, 0)),
        out_shape=jax.ShapeDtypeStruct((m, d), out_dtype),
        compiler_params=_params(("parallel",), VMEM_LIMIT),
        name="rmsnorm",
    )(x, g.reshape(1, d))


def _rmsnorm_split_kernel(x_ref, g_ref, a_ref, b_ref, *, n_first):
    x = x_ref[...]
    ms = jnp.mean(x * x, axis=-1, keepdims=True)
    y = x * lax.rsqrt(ms + NORM_EPS) * g_ref[...]
    i = pl.program_id(0)

    @pl.when(i < n_first)
    def _():
        a_ref[...] = y

    @pl.when(i >= n_first)
    def _():
        b_ref[...] = y


def _rmsnorm_split(x, g, m_first):
    m, d = x.shape
    tm = _tile(math.gcd(m_first, m - m_first), (512, 256, 128, 64, 32, 16, 8))
    n_first = m_first // tm
    return pl.pallas_call(
        functools.partial(_rmsnorm_split_kernel, n_first=n_first),
        grid=(m // tm,),
        in_specs=[pl.BlockSpec((tm, d), lambda i: (i, 0)), pl.BlockSpec((1, d), lambda i: (0, 0))],
        out_specs=[pl.BlockSpec((tm, d), lambda i: (jnp.minimum(i, n_first - 1), 0)),
                   pl.BlockSpec((tm, d), lambda i: (jnp.maximum(i - n_first, 0), 0))],
        out_shape=[jax.ShapeDtypeStruct((m_first, d), F32), jax.ShapeDtypeStruct((m - m_first, d), F32)],
        compiler_params=_params(("arbitrary",), VMEM_LIMIT),
        name="rmsnorm_split",
    )(x, g.reshape(1, d))


def _mm_kernel(x_ref, w_ref, o_ref):
    o_ref[...] = jnp.dot(x_ref[...], w_ref[...], preferred_element_type=F32).astype(o_ref.dtype)


def _mm_res_kernel(x_ref, w_ref, r_ref, o_ref):
    acc = jnp.dot(x_ref[...], w_ref[...], preferred_element_type=F32)
    k = pl.program_id(2)

    @pl.when(k == 0)
    def _():
        o_ref[...] = r_ref[...] + acc

    @pl.when(k != 0)
    def _():
        o_ref[...] += acc


def _swiglu_kernel(x_ref, wg_ref, wu_ref, o_ref):
    x = x_ref[...]
    g = jnp.dot(x, wg_ref[...], preferred_element_type=F32)
    u = jnp.dot(x, wu_ref[...], preferred_element_type=F32)
    o_ref[...] = (g * jax.nn.sigmoid(g) * u).astype(o_ref.dtype)


def _matmul(x, w, out_dtype):
    m, k = x.shape
    n = w.shape[1]
    tm = _tile(m, (1024, 512, 256, 128, 64, 32, 16, 8))
    tn = _tile(n, (1024, 512, 256, 128))
    return pl.pallas_call(
        _mm_kernel,
        grid=(m // tm, n // tn),
        in_specs=[pl.BlockSpec((tm, k), lambda i, j: (i, 0)), pl.BlockSpec((k, tn), lambda i, j: (0, j))],
        out_specs=pl.BlockSpec((tm, tn), lambda i, j: (i, j)),
        out_shape=jax.ShapeDtypeStruct((m, n), out_dtype),
        compiler_params=_params(("parallel", "parallel"), VMEM_LIMIT),
        name="matmul",
    )(x, w)


def _matmul_residual(x, w, res, tk):
    m, k = x.shape
    n = w.shape[1]
    tm = _tile(m, (1024, 512, 256, 128, 64, 32, 16, 8))
    tn = _tile(n, (1024, 512, 256, 128))
    return pl.pallas_call(
        _mm_res_kernel,
        grid=(m // tm, n // tn, k // tk),
        in_specs=[pl.BlockSpec((tm, tk), lambda i, j, kk: (i, kk)),
                  pl.BlockSpec((tk, tn), lambda i, j, kk: (kk, j)),
                  pl.BlockSpec((tm, tn), lambda i, j, kk: (i, j))],
        out_specs=pl.BlockSpec((tm, tn), lambda i, j, kk: (i, j)),
        out_shape=jax.ShapeDtypeStruct((m, n), F32),
        compiler_params=_params(("parallel", "parallel", "arbitrary"), VMEM_LIMIT),
        name="matmul_residual",
    )(x, w, res)


def _swiglu(x, wg, wu):
    m, k = x.shape
    n = wg.shape[1]
    tm = _tile(m, (1024, 512, 256, 128, 64, 32, 16, 8))
    tn = _tile(n, (512, 256, 128))
    return pl.pallas_call(
        _swiglu_kernel,
        grid=(m // tm, n // tn),
        in_specs=[pl.BlockSpec((tm, k), lambda i, j: (i, 0)),
                  pl.BlockSpec((k, tn), lambda i, j: (0, j)),
                  pl.BlockSpec((k, tn), lambda i, j: (0, j))],
        out_specs=pl.BlockSpec((tm, tn), lambda i, j: (i, j)),
        out_shape=jax.ShapeDtypeStruct((m, n), BF16),
        compiler_params=_params(("parallel", "parallel"), VMEM_LIMIT),
        name="swiglu",
    )(x, wg, wu)


def _log_sigmoid(x):
    return jnp.minimum(x, 0.0) - jnp.log1p(jnp.exp(-jnp.abs(x)))


def _split_bf16(x):
    hi = x.astype(BF16)
    lo = (x - hi.astype(F32)).astype(BF16)
    return hi, lo


def _dot_nt(a, b):
    return lax.dot_general(a, b, (((1,), (1,)), ((), ())), preferred_element_type=F32)


def _dot_tn(a, b):
    return lax.dot_general(a, b, (((0,), (0,)), ((), ())), preferred_element_type=F32)


def _gla_tables(chunk):
    levels = chunk.bit_length() - 1
    idx = np.arange(chunk)
    prefix = np.zeros((1 + 2 * levels, chunk, chunk), np.float32)
    prefix[0] = idx[None, :] <= idx[:, None]
    for lv in range(levels):
        m = 1 << lv
        mid = (idx // (2 * m)) * 2 * m + m
        for t in range(chunk):
            if t >= mid[t]:
                prefix[1 + lv, t, mid[t]:t + 1] = 1.0
            else:
                prefix[1 + levels + lv, t, t + 1:mid[t]] = 1.0
    x = idx[:, None] ^ idx[None, :]
    level = np.where(idx[:, None] == idx[None, :], levels, np.floor(np.log2(np.maximum(x, 1))).astype(np.int32))
    level = np.where(idx[None, :] > idx[:, None], -1, level).astype(np.int32)
    return prefix.reshape(-1, chunk), level, levels


def _gla_kernel(q_ref, k_ref, v_ref, g_ref, al_ref, wa_ref, ba_ref, ng_ref, pre_ref, lvl_ref, s0_ref,
                o_ref, sT_ref, s_scr, *, chunk, levels, n_chunks, n_steps):
    c = pl.program_id(2)

    @pl.when(c == 0)
    def _():
        s_scr[...] = s0_ref[...]

    prefix = pre_ref[...]
    level = lvl_ref[...]
    ng = ng_ref[...]
    x = jnp.dot(al_ref[...].astype(BF16), wa_ref[...], preferred_element_type=F32) + ba_ref[...]
    hi, lo = _split_bf16(_log_sigmoid(x) * (1.0 / GLA_TAU))
    width = GLA_HEADS_PER_STEP * GLA_DK

    for ci in range(n_chunks):
        rows = pl.ds(ci * chunk, chunk)
        r0 = ci * chunk
        parts = jnp.dot(prefix, jnp.concatenate([hi[r0:r0 + chunk], lo[r0:r0 + chunk]], axis=1),
                        preferred_element_type=F32)
        sums_all = parts[:, :width] + parts[:, width:]
        for j in range(GLA_HEADS_PER_STEP):
            kl = slice(j * GLA_DK, (j + 1) * GLA_DK)
            vl = slice(j * GLA_DV, (j + 1) * GLA_DV)
            sums = sums_all[:, kl]
            b = sums[:chunk]
            b_last = b[chunk - 1:chunk, :]
            decay = jnp.exp(sums[chunk:])
            q = q_ref[rows, kl] * (GLA_DK ** -0.5)
            k = k_ref[rows, kl]
            v = v_ref[rows, vl].astype(BF16)
            scores = jnp.where(level == levels, _dot_nt(q.astype(BF16), k.astype(BF16)), 0.0)
            for lv in range(levels):
                q_lv = (q * decay[lv * chunk:(lv + 1) * chunk]).astype(BF16)
                k_lv = (k * decay[(levels + lv) * chunk:(levels + lv + 1) * chunk]).astype(BF16)
                scores = jnp.where(level == lv, _dot_nt(q_lv, k_lv), scores)
            qd = (q * jnp.exp(b)).astype(BF16)
            s_t = s_scr[j]
            o = jnp.dot(scores.astype(BF16), v, preferred_element_type=F32) + _dot_nt(qd, s_t.astype(BF16))
            kdec = (k * jnp.exp(b_last - b)).astype(BF16)
            s_scr[j] = s_t * jnp.exp(b_last) + _dot_tn(v, kdec)
            ms = jnp.mean(o * o, axis=-1, keepdims=True)
            gg = g_ref[rows, vl]
            o_ref[rows, vl] = (o * lax.rsqrt(ms + NORM_EPS) * ng * (gg * jax.nn.sigmoid(gg))).astype(o_ref.dtype)

    @pl.when(c == n_steps - 1)
    def _():
        sT_ref[...] = s_scr[...]


def _gla(proj, row0, batch, seq, wa_pad, ba, norm_g, s0_t):
    chunk = min(GLA_CHUNK, seq)
    tb = _tile(seq, (256, 128, 64, 32, 16, 8))
    n_chunks = tb // chunk
    n_steps = seq // tb
    rb0 = row0 // tb
    hp = GLA_HEADS_PER_STEP
    prefix, level, levels = _gla_tables(chunk)

    def rows(b, c):
        return rb0 + b * n_steps + c

    fixed = lambda b, h, c: (0, 0)
    kern = functools.partial(_gla_kernel, chunk=chunk, levels=levels, n_chunks=n_chunks, n_steps=n_steps)
    return pl.pallas_call(
        kern,
        grid=(batch, GLA_HEADS // hp, n_steps),
        in_specs=[
            pl.BlockSpec((tb, hp * GLA_DK), lambda b, h, c: (rows(b, c), COL_GLA_Q // (hp * GLA_DK) + h)),
            pl.BlockSpec((tb, hp * GLA_DK), lambda b, h, c: (rows(b, c), COL_GLA_K // (hp * GLA_DK) + h)),
            pl.BlockSpec((tb, hp * GLA_DV), lambda b, h, c: (rows(b, c), COL_GLA_V // (hp * GLA_DV) + h)),
            pl.BlockSpec((tb, hp * GLA_DV), lambda b, h, c: (rows(b, c), COL_GLA_G // (hp * GLA_DV) + h)),
            pl.BlockSpec((tb, LANE), lambda b, h, c: (rows(b, c), COL_GLA_A // LANE)),
            pl.BlockSpec((LANE, hp * GLA_DK), lambda b, h, c: (0, h)),
            pl.BlockSpec((1, hp * GLA_DK), lambda b, h, c: (0, h)),
            pl.BlockSpec((1, GLA_DV), fixed),
            pl.BlockSpec(prefix.shape, fixed),
            pl.BlockSpec(level.shape, fixed),
            pl.BlockSpec((None, hp, GLA_DV, GLA_DK), lambda b, h, c: (b, h, 0, 0)),
        ],
        out_specs=[
            pl.BlockSpec((tb, hp * GLA_DV), lambda b, h, c: (b * n_steps + c, h)),
            pl.BlockSpec((None, hp, GLA_DV, GLA_DK), lambda b, h, c: (b, h, 0, 0)),
        ],
        out_shape=[
            jax.ShapeDtypeStruct((batch * seq, W_GLA), BF16),
            jax.ShapeDtypeStruct((batch, GLA_HEADS, GLA_DV, GLA_DK), F32),
        ],
        scratch_shapes=[pltpu.VMEM((hp, GLA_DV, GLA_DK), F32)],
        compiler_params=_params(("parallel", "parallel", "arbitrary"), VMEM_LIMIT),
        name="gla",
    )(proj, proj, proj, proj, proj, wa_pad, ba, norm_g, jnp.asarray(prefix, BF16), jnp.asarray(level), s0_t)


def _sb_suffix_and_total():
    r2 = lax.broadcasted_iota(jnp.int32, (SB_BLOCK, 2 * SB_BLOCK), 0)
    c2 = lax.broadcasted_iota(jnp.int32, (SB_BLOCK, 2 * SB_BLOCK), 1)
    return jnp.where((r2 > c2) | (c2 >= SB_BLOCK), 1.0, 0.0).astype(BF16)


def _sb_causal(n_rows):
    row = lax.broadcasted_iota(jnp.int32, (n_rows, SB_BLOCK), 0)
    col = lax.broadcasted_iota(jnp.int32, (n_rows, SB_BLOCK), 1)
    return col < row


def _head_lanes(j):
    return slice(j * SB_HEAD, (j + 1) * SB_HEAD)


def _sb_block(q, kblk, vblk, run, acc, suffix_and_total, causal):
    z = _dot_nt(q, kblk) * SB_SCALE
    sp = jnp.maximum(z, 0.0) + jnp.log1p(jnp.exp(-jnp.abs(z)))
    log_1m = -sp
    if causal is not None:
        log_1m = jnp.where(causal, log_1m, 0.0)
    log_beta = z - sp
    hi, lo = _split_bf16(log_1m)
    cs = (jnp.dot(hi, suffix_and_total, preferred_element_type=F32)
          + jnp.dot(lo, suffix_and_total, preferred_element_type=F32))
    att = jnp.exp(log_beta + cs[:, :SB_BLOCK] + run)
    if causal is not None:
        att = jnp.where(causal, att, 0.0)
    acc = acc + jnp.dot(att.astype(BF16), vblk, preferred_element_type=F32)
    return run + cs[:, SB_BLOCK:], acc


def _sb_sweep(qs, key_value_block, n_blocks, runs, accs, suffix_and_total):
    nh = len(qs)

    def run_max(runs):
        return functools.reduce(jnp.maximum, [jnp.max(r) for r in runs])

    def cond(carry):
        i, _, _, rmax = carry
        return jnp.logical_and(i < n_blocks, rmax > EXP_ZERO_BELOW)

    def body(carry):
        i, runs, accs, _ = carry
        rows = pl.ds(pl.multiple_of((n_blocks - 1 - i) * SB_BLOCK, SB_BLOCK), SB_BLOCK)
        new = [_sb_block(qs[j], *key_value_block(rows, j), runs[j], accs[j], suffix_and_total, None)
               for j in range(nh)]
        runs = tuple(n[0] for n in new)
        return i + 1, runs, tuple(n[1] for n in new), run_max(runs)

    _, _, accs, _ = lax.while_loop(cond, body, (jnp.int32(0), tuple(runs), tuple(accs), run_max(runs)))
    return accs


def _sb_finish(accs, g_ref, o_ref):
    for j, acc in enumerate(accs):
        ms = jnp.mean(acc * acc, axis=-1, keepdims=True)
        o_ref[:, _head_lanes(j)] = (acc * lax.rsqrt(ms + NORM_EPS) * g_ref[:, _head_lanes(j)]).astype(o_ref.dtype)


def _sb_prompt_kernel(q_ref, k_ref, v_ref, g_ref, o_ref):
    qi = pl.program_id(2)
    suffix_and_total = _sb_suffix_and_total()
    causal = _sb_causal(SB_BLOCK)
    zero = jnp.zeros((SB_BLOCK, SB_BLOCK), F32)

    def key_value_block(rows, j):
        return k_ref[rows, _head_lanes(j)].astype(BF16), v_ref[rows, _head_lanes(j)].astype(BF16)

    diag = pl.ds(pl.multiple_of(qi * SB_BLOCK, SB_BLOCK), SB_BLOCK)
    qs, runs, accs = [], [], []
    for j in range(SB_HEADS_PER_STEP):
        q = q_ref[:, _head_lanes(j)].astype(BF16)
        run, acc = _sb_block(q, *key_value_block(diag, j), zero, zero, suffix_and_total, causal)
        qs.append(q), runs.append(run), accs.append(acc)
    _sb_finish(_sb_sweep(qs, key_value_block, qi, runs, accs, suffix_and_total), g_ref, o_ref)


def _sb_prompt(proj, batch, seq, norm_g):
    nq = seq // SB_BLOCK
    wide = SB_HEADS_PER_STEP * SB_HEAD
    return pl.pallas_call(
        _sb_prompt_kernel,
        grid=(batch, SB_HEADS // SB_HEADS_PER_STEP, nq),
        in_specs=[
            pl.BlockSpec((SB_BLOCK, wide), lambda b, h, i: (b * nq + i, COL_SB_Q // wide + h)),
            pl.BlockSpec((seq, wide), lambda b, h, i: (b, COL_SB_K // wide + h)),
            pl.BlockSpec((seq, wide), lambda b, h, i: (b, COL_SB_V // wide + h)),
            pl.BlockSpec((1, wide), lambda b, h, i: (0, h)),
        ],
        out_specs=pl.BlockSpec((SB_BLOCK, wide), lambda b, h, i: (b * nq + i, h)),
        out_shape=jax.ShapeDtypeStruct((batch * seq, W_SB), BF16),
        compiler_params=_params(("parallel", "parallel", "arbitrary"), VMEM_LIMIT),
        name="sb_prompt",
    )(proj, proj, proj, norm_g)


def _sb_sample_kernel(q_ref, k_ref, v_ref, kp_ref, vp_ref, g_ref, o_ref, *, seq, n_past_blocks):
    suffix_and_total = _sb_suffix_and_total()
    causal = _sb_causal(seq)
    pad = jnp.zeros((SB_BLOCK - seq, SB_HEAD), BF16)
    zero = jnp.zeros((seq, SB_BLOCK), F32)

    def past_block(rows, j):
        return kp_ref[rows, j, :].astype(BF16), vp_ref[rows, j, :].astype(BF16)

    qs, runs, accs = [], [], []
    for j in range(SB_HEADS):
        q = q_ref[:, _head_lanes(j)].astype(BF16)
        kblk = jnp.concatenate([k_ref[:, _head_lanes(j)].astype(BF16), pad], axis=0)
        vblk = jnp.concatenate([v_ref[:, _head_lanes(j)].astype(BF16), pad], axis=0)
        run, acc = _sb_block(q, kblk, vblk, zero, zero, suffix_and_total, causal)
        qs.append(q), runs.append(run), accs.append(acc)
    _sb_finish(_sb_sweep(qs, past_block, n_past_blocks, runs, accs, suffix_and_total), g_ref, o_ref)


def _sb_sample(proj, row0, batch, seq, layer, cache_k, cache_v, norm_g):
    past = cache_k.shape[2]
    rb0 = row0 // seq
    kern = functools.partial(_sb_sample_kernel, seq=seq, n_past_blocks=past // SB_BLOCK)
    cache_spec = pl.BlockSpec((None, None, past, SB_HEADS, SB_HEAD), lambda b: (layer, b, 0, 0, 0))
    return pl.pallas_call(
        kern,
        grid=(batch,),
        in_specs=[
            pl.BlockSpec((seq, W_SB), lambda b: (rb0 + b, COL_SB_Q // W_SB)),
            pl.BlockSpec((seq, W_SB), lambda b: (rb0 + b, COL_SB_K // W_SB)),
            pl.BlockSpec((seq, W_SB), lambda b: (rb0 + b, COL_SB_V // W_SB)),
            cache_spec, cache_spec,
            pl.BlockSpec((1, W_SB), lambda b: (0, 0)),
        ],
        out_specs=pl.BlockSpec((seq, W_SB), lambda b: (b, 0)),
        out_shape=jax.ShapeDtypeStruct((batch * seq, W_SB), BF16),
        compiler_params=_params(("parallel",), VMEM_LIMIT),
        name="sb_sample",
    )(proj, proj, proj, cache_k, cache_v, norm_g)


def _rwkv_prep_kernel(p_ref, bnd_ref, mu_ref, w0_ref, a0_ref, w2_ref, a2_ref, g2_ref,
                      r_out, w_out, k_out, v_out, a_out, g_out):
    p = p_ref[...]
    first = lax.broadcasted_iota(jnp.int32, p.shape, 0) == 0
    prev = jnp.where(first, bnd_ref[...], pltpu.roll(p, 1, 0))
    pm = p + (prev - p) * mu_ref[...]
    o = 0
    r = pm[:, o:o + W_RWKV]; o += W_RWKV
    w_low = pm[:, o:o + D_LORA]; o += D_LORA
    k = pm[:, o:o + W_RWKV]; o += W_RWKV
    v = pm[:, o:o + W_RWKV]; o += W_RWKV
    a_low = pm[:, o:o + D_LORA]; o += D_LORA
    g_low = pm[:, o:o + D_GATE_PAD]
    log_w = -math.exp(-0.5) * jax.nn.sigmoid(
        w0_ref[...] + jnp.dot(jnp.tanh(w_low).astype(BF16), w2_ref[...], preferred_element_type=F32))
    a = jax.nn.sigmoid(a0_ref[...] + jnp.dot(a_low.astype(BF16), a2_ref[...], preferred_element_type=F32))
    g = jnp.dot(jax.nn.sigmoid(g_low).astype(BF16), g2_ref[...], preferred_element_type=F32)
    r_out[...] = r
    w_out[...] = jnp.exp(log_w)
    k_out[...] = k
    v_out[...] = v
    a_out[...] = a
    g_out[...] = g


def _rwkv_prep(proj, row0, batch, seq, shift, mu, w0, a0, w2, a2, g2):
    tm = _tile(seq, (128, 64, 32, 16, 8))
    nt = seq // tm
    m = batch * seq
    rb0 = row0 // tm
    last = proj[row0 + tm - 1:row0 + m:tm, :P_RWKV_PAD].reshape(batch, nt, P_RWKV_PAD)
    bnd = jnp.concatenate([shift[:, None, :], last[:, :nt - 1]], axis=1).reshape(batch * nt, 1, P_RWKV_PAD)
    row = lambda i: (i, 0)
    fixed = lambda i: (0, 0)
    vec = pl.BlockSpec((1, W_RWKV), fixed)
    out = pl.BlockSpec((tm, W_RWKV), row)
    return pl.pallas_call(
        _rwkv_prep_kernel,
        grid=(m // tm,),
        in_specs=[pl.BlockSpec((tm, P_RWKV_PAD), lambda i: (rb0 + i, 0)),
                  pl.BlockSpec((None, 1, P_RWKV_PAD), lambda i: (i, 0, 0)),
                  pl.BlockSpec((1, P_RWKV_PAD), fixed), vec, vec,
                  pl.BlockSpec((D_LORA, W_RWKV), fixed), pl.BlockSpec((D_LORA, W_RWKV), fixed),
                  pl.BlockSpec((D_GATE_PAD, W_RWKV), fixed)],
        out_specs=[out] * 6,
        out_shape=[jax.ShapeDtypeStruct((m, W_RWKV), F32)] * 6,
        compiler_params=_params(("parallel",), VMEM_LIMIT),
        name="rwkv_prep",
    )(proj, bnd, mu, w0, a0, w2, a2, g2)


def _rwkv_scan_kernel(r_ref, w_ref, k_ref, v_ref, a_ref, rk_ref, lng_ref, lnb_ref, kkp_ref, kap_ref, s0_ref,
                      z_ref, sT_ref, s_scr, kk_scr, b_scr, w_scr, k_scr, r_scr, v_scr, y_scr,
                      *, t_block, n_steps, vs):
    c = pl.program_id(1)
    n = RWKV_HEAD
    nv = n // vs
    lanes_in = r_ref.shape[-1]
    lanes_work = lanes_in * vs

    def dup(x):
        return x if vs == 1 else jnp.concatenate([x] * vs, axis=1)

    @pl.when(c == 0)
    def _():
        s_scr[...] = s0_ref[...]

    def step(t, carry):
        k_raw = k_ref[t]
        a = a_ref[t]
        kk_raw = k_raw * kkp_ref[...]
        norm = jnp.sqrt(jnp.sum(kk_raw * kk_raw, axis=0, keepdims=True))
        kk = kk_raw / jnp.maximum(norm, 1e-12)
        k_t = k_raw * (1.0 + (a - 1.0) * kap_ref[...])
        kk_scr[...] = dup(kk)
        b_scr[...] = dup(kk * a)
        w_scr[...] = dup(w_ref[t])
        k_scr[...] = dup(k_t)
        r_scr[...] = dup(r_ref[t])
        v_t = v_ref[t]
        v_scr[...] = v_t if vs == 1 else jnp.concatenate([v_t[g * nv:(g + 1) * nv] for g in range(vs)], axis=1)

        for lg in range(lanes_work // LANE):
            ls = pl.ds(lg * LANE, LANE)

            def value_rows(vo, carry2):
                v0 = pl.multiple_of(vo * SUBLANE, SUBLANE)
                v_tile = v_scr[pl.ds(v0, SUBLANE), ls]
                y_rows = []
                for vi in range(SUBLANE):
                    s_v = s_scr[v0 + vi, :, ls]
                    sa = jnp.sum(s_v * kk_scr[:, ls], axis=0, keepdims=True)
                    s_new = s_v * w_scr[:, ls] - sa * b_scr[:, ls] + v_tile[vi:vi + 1, :] * k_scr[:, ls]
                    s_scr[v0 + vi, :, ls] = s_new
                    y_rows.append(jnp.sum(s_new * r_scr[:, ls], axis=0, keepdims=True))
                y_scr[pl.ds(v0, SUBLANE), ls] = jnp.concatenate(y_rows, axis=0)
                return carry2

            lax.fori_loop(0, nv // SUBLANE, value_rows, 0)

        y2 = y_scr[...]
        y = y2 if vs == 1 else jnp.concatenate([y2[:, g * lanes_in:(g + 1) * lanes_in] for g in range(vs)], axis=0)
        mean = jnp.mean(y, axis=0, keepdims=True)
        d = y - mean
        var = jnp.mean(d * d, axis=0, keepdims=True)
        bonus = jnp.sum(r_ref[t] * k_t * rk_ref[...], axis=0, keepdims=True) * v_t
        z_ref[t] = d * lax.rsqrt(var + RWKV_GN_EPS) * lng_ref[...] + lnb_ref[...] + bonus
        return carry

    lax.fori_loop(0, t_block, step, 0)

    @pl.when(c == n_steps - 1)
    def _():
        sT_ref[...] = s_scr[...]


def _rwkv_scan(r, w, k, v, a, rk, lng, lnb, kkp, kap, s0, vs):
    seq, n, lanes = r.shape
    nv = n // vs
    t_block = _tile(seq, (8, 4, 2, 1))
    n_steps = seq // t_block
    lb = lanes if vs > 1 else _tile(lanes, (3 * LANE, 2 * LANE, LANE))
    lbw = lb * vs
    seq_spec = pl.BlockSpec((t_block, n, lb), lambda g, c: (c, 0, g))
    vec_spec = pl.BlockSpec((n, lb), lambda g, c: (0, g))
    st_spec = pl.BlockSpec((nv, n, lbw), lambda g, c: (0, 0, g))
    kern = functools.partial(_rwkv_scan_kernel, t_block=t_block, n_steps=n_steps, vs=vs)
    work = lambda rows: pltpu.VMEM((rows, lbw), F32)
    return pl.pallas_call(
        kern,
        grid=(lanes // lb, n_steps),
        in_specs=[seq_spec] * 5 + [vec_spec] * 5 + [st_spec],
        out_specs=[seq_spec, st_spec],
        out_shape=[jax.ShapeDtypeStruct((seq, n, lanes), F32), jax.ShapeDtypeStruct((nv, n, lanes * vs), F32)],
        scratch_shapes=[pltpu.VMEM((nv, n, lbw), F32), work(n), work(n), work(n), work(n), work(n),
                        work(nv), work(nv)],
        compiler_params=_params(("parallel", "arbitrary"), VMEM_LIMIT),
        name="rwkv_scan",
    )(r, w, k, v, a, rk, lng, lnb, kkp, kap, s0)


def _merge_kernel(gla_p, z_p, g_p, sb_p, gla_s, z_s, g_s, sb_s, o_ref, *, n_prompt_tiles):
    def write(gla_ref, z_ref, g_ref, sb_ref):
        o_ref[:, 0:W_GLA] = gla_ref[...]
        o_ref[:, W_GLA:W_GLA + W_RWKV] = (z_ref[...] * g_ref[...]).astype(o_ref.dtype)
        o_ref[:, W_GLA + W_RWKV:D_MODEL] = sb_ref[...]

    i = pl.program_id(0)

    @pl.when(i < n_prompt_tiles)
    def _():
        write(gla_p, z_p, g_p, sb_p)

    @pl.when(i >= n_prompt_tiles)
    def _():
        write(gla_s, z_s, g_s, sb_s)


def _merge(prompt, sample):
    mp, ms = prompt[1].shape[0], sample[1].shape[0]
    tm = _tile(math.gcd(mp, ms), (512, 256, 128, 64, 32, 16))
    n_p, n_s = mp // tm, ms // tm
    widths = (W_GLA, W_RWKV, W_RWKV, W_SB)
    p_specs = [pl.BlockSpec((tm, wd), lambda i: (jnp.minimum(i, n_p - 1), 0)) for wd in widths]
    s_specs = [pl.BlockSpec((tm, wd), lambda i: (jnp.maximum(i - n_p, 0), 0)) for wd in widths]
    return pl.pallas_call(
        functools.partial(_merge_kernel, n_prompt_tiles=n_p),
        grid=(n_p + n_s,),
        in_specs=p_specs + s_specs,
        out_specs=pl.BlockSpec((tm, D_MODEL), lambda i: (i, 0)),
        out_shape=jax.ShapeDtypeStruct((mp + ms, D_MODEL), BF16),
        compiler_params=_params(("parallel",), VMEM_LIMIT),
        name="merge",
    )(*prompt, *sample)


def _to_scan_layout(x, batch, seq):
    x = x.reshape(batch, seq, RWKV_HEADS, RWKV_HEAD).transpose(1, 3, 0, 2)
    return x.reshape(seq, RWKV_HEAD, batch * RWKV_HEADS)


def _from_scan_layout(z, batch, seq):
    z = z.reshape(seq, RWKV_HEAD, batch, RWKV_HEADS)
    return z.transpose(2, 0, 3, 1).reshape(batch * seq, W_RWKV)


def _head_vec_to_scan_layout(x, batch):
    x = x.reshape(RWKV_HEADS, RWKV_HEAD).T
    return jnp.tile(x[:, None, :], (1, batch, 1)).reshape(RWKV_HEAD, batch * RWKV_HEADS)


def _rwkv_group(seqs, batch, seq, params, s0):
    lanes = batch * RWKV_HEADS
    n = RWKV_HEAD
    vs = next(f for f in (1, 2, 4, 8) if (lanes * f) % LANE == 0)
    nv = n // vs
    s0_l = s0.transpose(2, 3, 0, 1).reshape(vs, nv, n, lanes).transpose(1, 2, 0, 3).reshape(nv, n, vs * lanes)
    ins = [_to_scan_layout(x, batch, seq) for x in seqs]
    vecs = [_head_vec_to_scan_layout(x, batch) for x in params]
    z, s_t = _rwkv_scan(*ins, *vecs, s0_l, vs)
    s_new = s_t.reshape(nv, n, vs, lanes).transpose(2, 0, 1, 3).reshape(n, n, batch, RWKV_HEADS).transpose(2, 3, 0, 1)
    return _from_scan_layout(z, batch, seq), s_new


def _rearrange_w_in(w):
    d = w.shape[0]
    w = w.astype(BF16)
    z = lambda n: jnp.zeros((d, n), BF16)
    gla, rwkv, sb = w[:, :P_GLA], w[:, P_GLA:P_GLA + P_RWKV], w[:, P_GLA + P_RWKV:]
    parts = [rwkv, z(P_RWKV_PAD - P_RWKV), gla[:, :GLA_QK], sb,
             gla[:, 2 * GLA_QK:2 * GLA_QK + W_GLA], gla[:, 2 * GLA_QK + W_GLA:2 * GLA_QK + 2 * W_GLA],
             gla[:, GLA_QK:2 * GLA_QK], gla[:, 2 * GLA_QK + 2 * W_GLA:], z(LANE - GLA_LOWRANK)]
    width = sum(p.shape[1] for p in parts)
    return jnp.concatenate(parts + [z(P_PAD - width)], axis=1)


def kernel(x_prompt, x_sample, state_gla, state_rwkv, state_rwkv_shift, cache_sb_k, cache_sb_v, norm1_g, w_in, gla_wa2, gla_ba, gla_norm_g, rwkv_mu, rwkv_w0, rwkv_w2, rwkv_a0, rwkv_a2, rwkv_g2, rwkv_k_k, rwkv_k_a, rwkv_r_k, rwkv_ln_g, rwkv_ln_b, sb_norm_g, w_out, norm2_g, w_gate, w_up, w_down, final_g):
    bp, tp, d = x_prompt.shape
    bs, ts, _ = x_sample.shape
    depth = w_in.shape[0]
    mp, ms = bp * tp, bs * ts
    x = jnp.concatenate([x_prompt.reshape(mp, d), x_sample.reshape(ms, d)], axis=0)

    outs = {name: [] for name in ("gla_p", "gla_s", "rw_p", "rw_s", "sh_p", "sh_s", "k_p", "k_s", "v_p", "v_s")}
    row1 = lambda a: a.reshape(1, -1)
    for l in range(depth):
        h = _rmsnorm(x, norm1_g[l], BF16)
        proj = _matmul(h, _rearrange_w_in(w_in[l]), F32)
        outs["sh_p"].append(proj[tp - 1:mp:tp, :P_RWKV].reshape(bp, 1, P_RWKV))
        outs["sh_s"].append(proj[mp + ts - 1::ts, :P_RWKV].reshape(bs, 1, P_RWKV))
        for nm, c0 in (("k", COL_SB_K), ("v", COL_SB_V)):
            outs[nm + "_p"].append(proj[:mp, c0:c0 + W_SB].reshape(bp, tp, SB_HEADS, SB_HEAD))
            outs[nm + "_s"].append(proj[mp:, c0:c0 + W_SB].reshape(bs, ts, SB_HEADS, SB_HEAD))

        wa_pad = jnp.pad(gla_wa2[l], ((0, LANE - GLA_LOWRANK), (0, 0))).astype(BF16)
        gla_args = (wa_pad, row1(gla_ba[l]), row1(gla_norm_g[l]))
        o_gla_p, sg_p = _gla(proj, 0, bp, tp, *gla_args, jnp.zeros((bp, GLA_HEADS, GLA_DV, GLA_DK), F32))
        o_gla_s, sg_s = _gla(proj, mp, bs, ts, *gla_args, state_gla[l].swapaxes(-1, -2))
        outs["gla_p"].append(sg_p.swapaxes(-1, -2))
        outs["gla_s"].append(sg_s.swapaxes(-1, -2))

        lane_pad = (0, P_RWKV_PAD - P_RWKV)
        prep_args = (row1(jnp.pad(rwkv_mu[l], lane_pad)), row1(rwkv_w0[l]), row1(rwkv_a0[l]),
                     rwkv_w2[l].astype(BF16), rwkv_a2[l].astype(BF16),
                     jnp.pad(rwkv_g2[l], ((0, D_GATE_PAD - D_GATE_LORA), (0, 0))).astype(BF16))
        vec_args = (rwkv_r_k[l].reshape(-1), rwkv_ln_g[l], rwkv_ln_b[l], rwkv_k_k[l], rwkv_k_a[l])
        *seqs_p, gate_p = _rwkv_prep(proj, 0, bp, tp, jnp.zeros((bp, P_RWKV_PAD), F32), *prep_args)
        *seqs_s, gate_s = _rwkv_prep(proj, mp, bs, ts, jnp.pad(state_rwkv_shift[l][:, 0, :], ((0, 0), lane_pad)),
                                     *prep_args)
        z_p, sr_p = _rwkv_group(seqs_p, bp, tp, vec_args, jnp.zeros((bp, RWKV_HEADS, RWKV_HEAD, RWKV_HEAD), F32))
        z_s, sr_s = _rwkv_group(seqs_s, bs, ts, vec_args, state_rwkv[l])
        outs["rw_p"].append(sr_p)
        outs["rw_s"].append(sr_s)

        sbg = row1(sb_norm_g[l])
        o_sb_p = _sb_prompt(proj, bp, tp, sbg)
        o_sb_s = _sb_sample(proj, mp, bs, ts, l, cache_sb_k, cache_sb_v, sbg)

        merged = _merge((o_gla_p, z_p, gate_p, o_sb_p), (o_gla_s, z_s, gate_s, o_sb_s))
        x = _matmul_residual(merged, w_out[l].astype(BF16), x, tk=d // 2)

        h2 = _rmsnorm(x, norm2_g[l], BF16)
        ff_pad = ((0, 0), (0, D_FF_PAD - D_FF))
        act = _swiglu(h2, jnp.pad(w_gate[l].astype(BF16), ff_pad), jnp.pad(w_up[l].astype(BF16), ff_pad))
        w_dn = jnp.pad(w_down[l].astype(BF16), ((0, D_FF_PAD - D_FF), (0, 0)))
        x = _matmul_residual(act, w_dn, x, tk=D_FF_PAD // 4)

    y_p, y_s = _rmsnorm_split(x, final_g, mp)
    sd = state_gla.dtype
    st = lambda name, dt=None: jnp.stack(outs[name]) if dt is None else jnp.stack(outs[name]).astype(dt)
    return (y_p.reshape(bp, tp, d), y_s.reshape(bs, ts, d),
            st("gla_p", sd), st("gla_s", sd), st("rw_p", sd), st("rw_s", sd),
            st("sh_p"), st("sh_s"), st("k_p"), st("k_s"), st("v_p"), st("v_s"))
```

```python
import functools
import math

import jax
import jax.numpy as jnp
import numpy as np
from jax import lax
from jax.experimental import pallas as pl
from jax.experimental.pallas import tpu as pltpu

F32 = jnp.float32
BF16 = jnp.bfloat16

D_MODEL = 4096
NORM_EPS = 1e-5
W_GLA = 1536
W_RWKV = 1536
W_SB = 1024
GLA_HEADS = 6
GLA_DV = 256
GLA_DK = 128
GLA_QK = 768
GLA_LOWRANK = 16
GLA_TAU = 16.0
GLA_CHUNK = 64
GLA_HEADS_PER_STEP = 3
RWKV_HEAD = 64
RWKV_HEADS = 24
D_LORA = 128
D_GATE_LORA = 480
D_GATE_PAD = 512
RWKV_GN_EPS = 64e-5
SB_HEAD = 128
SB_HEADS = 8
SB_SCALE = SB_HEAD ** -0.5
SB_BLOCK = 128
SB_HEADS_PER_STEP = 8
D_FF = 11008
D_FF_PAD = 11264
P_GLA = 4624
P_RWKV = 5344
P_RWKV_PAD = 5376
P_SB = 3072

COL_RWKV = 0
COL_GLA_Q = P_RWKV_PAD
COL_SB_Q = COL_GLA_Q + GLA_QK
COL_SB_K = COL_SB_Q + W_SB
COL_SB_V = COL_SB_K + W_SB
COL_GLA_V = COL_SB_V + W_SB
COL_GLA_G = COL_GLA_V + W_GLA
COL_GLA_K = COL_GLA_G + W_GLA
COL_GLA_A = COL_GLA_K + GLA_QK
P_PAD = 13312

LANE = 128
SUBLANE = 8
VMEM_LIMIT = 56 * 1024 * 1024

EXP_ZERO_BELOW = -104.0


def _tile(n, candidates):
    for c in candidates:
        if n % c == 0:
            return c
    return n


def _params(sem, vmem=None):
    return pltpu.CompilerParams(dimension_semantics=sem, vmem_limit_bytes=vmem)


def _rmsnorm_kernel(x_ref, g_ref, o_ref):
    x = x_ref[...]
    ms = jnp.mean(x * x, axis=-1, keepdims=True)
    o_ref[...] = (x * lax.rsqrt(ms + NORM_EPS) * g_ref[...]).astype(o_ref.dtype)


def _rmsnorm(x, g, out_dtype):
    m, d = x.shape
    tm = _tile(m, (512, 256, 128, 64, 32, 16, 8))
    return pl.pallas_call(
        _rmsnorm_kernel,
        grid=(m // tm,),
        in_specs=[pl.BlockSpec((tm, d), lambda i: (i, 0)), pl.BlockSpec((1, d), lambda i: (0, 0))],
        out_specs=pl.BlockSpec((tm, d), lambda i: (i, 0)),
        out_shape=jax.ShapeDtypeStruct((m, d), out_dtype),
        compiler_params=_params(("parallel",), VMEM_LIMIT),
        name="rmsnorm",
    )(x, g.reshape(1, d))


def _rmsnorm_split_kernel(x_ref, g_ref, a_ref, b_ref, *, n_first):
    x = x_ref[...]
    ms = jnp.mean(x * x, axis=-1, keepdims=True)
    y = x * lax.rsqrt(ms + NORM_EPS) * g_ref[...]
    i = pl.program_id(0)

    @pl.when(i < n_first)
    def _():
        a_ref[...] = y

    @pl.when(i >= n_first)
    def _():
        b_ref[...] = y


def _rmsnorm_split(x, g, m_first):
    m, d = x.shape
    tm = _tile(math.gcd(m_first, m - m_first), (512, 256, 128, 64, 32, 16, 8))
    n_first = m_first // tm
    return pl.pallas_call(
        functools.partial(_rmsnorm_split_kernel, n_first=n_first),
        grid=(m // tm,),
        in_specs=[pl.BlockSpec((tm, d), lambda i: (i, 0)), pl.BlockSpec((1, d), lambda i: (0, 0))],
        out_specs=[pl.BlockSpec((tm, d), lambda i: (jnp.minimum(i, n_first - 1), 0)),
                   pl.BlockSpec((tm, d), lambda i: (jnp.maximum(i - n_first, 0), 0))],
        out_shape=[jax.ShapeDtypeStruct((m_first, d), F32), jax.ShapeDtypeStruct((m - m_first, d), F32)],
        compiler_params=_params(("arbitrary",), VMEM_LIMIT),
        name="rmsnorm_split",
    )(x, g.reshape(1, d))


def _mm_kernel(x_ref, w_ref, o_ref):
    o_ref[...] = jnp.dot(x_ref[...], w_ref[...], preferred_element_type=F32).astype(o_ref.dtype)


def _mm_res_kernel(x_ref, w_ref, r_ref, o_ref):
    acc = jnp.dot(x_ref[...], w_ref[...], preferred_element_type=F32)
    k = pl.program_id(2)

    @pl.when(k == 0)
    def _():
        o_ref[...] = r_ref[...] + acc

    @pl.when(k != 0)
    def _():
        o_ref[...] += acc


def _swiglu_kernel(x_ref, wg_ref, wu_ref, o_ref):
    x = x_ref[...]
    g = jnp.dot(x, wg_ref[...], preferred_element_type=F32)
    u = jnp.dot(x, wu_ref[...], preferred_element_type=F32)
    o_ref[...] = (g * jax.nn.sigmoid(g) * u).astype(o_ref.dtype)


def _matmul(x, w, out_dtype):
    m, k = x.shape
    n = w.shape[1]
    tm = _tile(m, (1024, 512, 256, 128, 64, 32, 16, 8))
    tn = _tile(n, (1024, 512, 256, 128))
    return pl.pallas_call(
        _mm_kernel,
        grid=(m // tm, n // tn),
        in_specs=[pl.BlockSpec((tm, k), lambda i, j: (i, 0)), pl.BlockSpec((k, tn), lambda i, j: (0, j))],
        out_specs=pl.BlockSpec((tm, tn), lambda i, j: (i, j)),
        out_shape=jax.ShapeDtypeStruct((m, n), out_dtype),
        compiler_params=_params(("parallel", "parallel"), VMEM_LIMIT),
        name="matmul",
    )(x, w)


def _matmul_residual(x, w, res, tk):
    m, k = x.shape
    n = w.shape[1]
    tm = _tile(m, (1024, 512, 256, 128, 64, 32, 16, 8))
    tn = _tile(n, (1024, 512, 256, 128))
    return pl.pallas_call(
        _mm_res_kernel,
        grid=(m // tm, n // tn, k // tk),
        in_specs=[pl.BlockSpec((tm, tk), lambda i, j, kk: (i, kk)),
                  pl.BlockSpec((tk, tn), lambda i, j, kk: (kk, j)),
                  pl.BlockSpec((tm, tn), lambda i, j, kk: (i, j))],
        out_specs=pl.BlockSpec((tm, tn), lambda i, j, kk: (i, j)),
        out_shape=jax.ShapeDtypeStruct((m, n), F32),
        compiler_params=_params(("parallel", "parallel", "arbitrary"), VMEM_LIMIT),
        name="matmul_residual",
    )(x, w, res)


def _swiglu(x, wg, wu):
    m, k = x.shape
    n = wg.shape[1]
    tm = _tile(m, (1024, 512, 256, 128, 64, 32, 16, 8))
    tn = _tile(n, (512, 256, 128))
    return pl.pallas_call(
        _swiglu_kernel,
        grid=(m // tm, n // tn),
        in_specs=[pl.BlockSpec((tm, k), lambda i, j: (i, 0)),
                  pl.BlockSpec((k, tn), lambda i, j: (0, j)),
                  pl.BlockSpec((k, tn), lambda i, j: (0, j))],
        out_specs=pl.BlockSpec((tm, tn), lambda i, j: (i, j)),
        out_shape=jax.ShapeDtypeStruct((m, n), BF16),
        compiler_params=_params(("parallel", "parallel"), VMEM_LIMIT),
        name="swiglu",
    )(x, wg, wu)


def _log_sigmoid(x):
    return jnp.minimum(x, 0.0) - jnp.log1p(jnp.exp(-jnp.abs(x)))


def _split_bf16(x):
    hi = x.astype(BF16)
    lo = (x - hi.astype(F32)).astype(BF16)
    return hi, lo


def _dot_nt(a, b):
    return lax.dot_general(a, b, (((1,), (1,)), ((), ())), preferred_element_type=F32)


def _dot_tn(a, b):
    return lax.dot_general(a, b, (((0,), (0,)), ((), ())), preferred_element_type=F32)


def _gla_tables(chunk):
    levels = chunk.bit_length() - 1
    idx = np.arange(chunk)
    prefix = np.zeros((1 + 2 * levels, chunk, chunk), np.float32)
    prefix[0] = idx[None, :] <= idx[:, None]
    for lv in range(levels):
        m = 1 << lv
        mid = (idx // (2 * m)) * 2 * m + m
        for t in range(chunk):
            if t >= mid[t]:
                prefix[1 + lv, t, mid[t]:t + 1] = 1.0
            else:
                prefix[1 + levels + lv, t, t + 1:mid[t]] = 1.0
    x = idx[:, None] ^ idx[None, :]
    level = np.where(idx[:, None] == idx[None, :], levels, np.floor(np.log2(np.maximum(x, 1))).astype(np.int32))
    level = np.where(idx[None, :] > idx[:, None], -1, level).astype(np.int32)
    return prefix.reshape(-1, chunk), level, levels


def _gla_kernel(q_ref, k_ref, v_ref, g_ref, al_ref, wa_ref, ba_ref, ng_ref, pre_ref, lvl_ref, s0_ref,
                o_ref, sT_ref, s_scr, *, chunk, levels, n_chunks, n_steps):
    c = pl.program_id(2)

    @pl.when(c == 0)
    def _():
        s_scr[...] = s0_ref[...]

    prefix = pre_ref[...]
    level = lvl_ref[...]
    ng = ng_ref[...]
    x = jnp.dot(al_ref[...].astype(BF16), wa_ref[...], preferred_element_type=F32) + ba_ref[...]
    hi, lo = _split_bf16(_log_sigmoid(x) * (1.0 / GLA_TAU))
    width = GLA_HEADS_PER_STEP * GLA_DK

    for ci in range(n_chunks):
        rows = pl.ds(ci * chunk, chunk)
        r0 = ci * chunk
        parts = jnp.dot(prefix, jnp.concatenate([hi[r0:r0 + chunk], lo[r0:r0 + chunk]], axis=1),
                        preferred_element_type=F32)
        sums_all = parts[:, :width] + parts[:, width:]
        for j in range(GLA_HEADS_PER_STEP):
            kl = slice(j * GLA_DK, (j + 1) * GLA_DK)
            vl = slice(j * GLA_DV, (j + 1) * GLA_DV)
            sums = sums_all[:, kl]
            b = sums[:chunk]
            b_last = b[chunk - 1:chunk, :]
            decay = jnp.exp(sums[chunk:])
            q = q_ref[rows, kl] * (GLA_DK ** -0.5)
            k = k_ref[rows, kl]
            v = v_ref[rows, vl].astype(BF16)
            scores = jnp.where(level == levels, _dot_nt(q.astype(BF16), k.astype(BF16)), 0.0)
            for lv in range(levels):
                q_lv = (q * decay[lv * chunk:(lv + 1) * chunk]).astype(BF16)
                k_lv = (k * decay[(levels + lv) * chunk:(levels + lv + 1) * chunk]).astype(BF16)
                scores = jnp.where(level == lv, _dot_nt(q_lv, k_lv), scores)
            qd = (q * jnp.exp(b)).astype(BF16)
            s_t = s_scr[j]
            o = jnp.dot(scores.astype(BF16), v, preferred_element_type=F32) + _dot_nt(qd, s_t.astype(BF16))
            kdec = (k * jnp.exp(b_last - b)).astype(BF16)
            s_scr[j] = s_t * jnp.exp(b_last) + _dot_tn(v, kdec)
            ms = jnp.mean(o * o, axis=-1, keepdims=True)
            gg = g_ref[rows, vl]
            o_ref[rows, vl] = (o * lax.rsqrt(ms + NORM_EPS) * ng * (gg * jax.nn.sigmoid(gg))).astype(o_ref.dtype)

    @pl.when(c == n_steps - 1)
    def _():
        sT_ref[...] = s_scr[...]


def _gla(proj, row0, batch, seq, wa_pad, ba, norm_g, s0_t):
    chunk = min(GLA_CHUNK, seq)
    tb = _tile(seq, (256, 128, 64, 32, 16, 8))
    n_chunks = tb // chunk
    n_steps = seq // tb
    rb0 = row0 // tb
    hp = GLA_HEADS_PER_STEP
    prefix, level, levels = _gla_tables(chunk)

    def rows(b, c):
        return rb0 + b * n_steps + c

    fixed = lambda b, h, c: (0, 0)
    kern = functools.partial(_gla_kernel, chunk=chunk, levels=levels, n_chunks=n_chunks, n_steps=n_steps)
    return pl.pallas_call(
        kern,
        grid=(batch, GLA_HEADS // hp, n_steps),
        in_specs=[
            pl.BlockSpec((tb, hp * GLA_DK), lambda b, h, c: (rows(b, c), COL_GLA_Q // (hp * GLA_DK) + h)),
            pl.BlockSpec((tb, hp * GLA_DK), lambda b, h, c: (rows(b, c), COL_GLA_K // (hp * GLA_DK) + h)),
            pl.BlockSpec((tb, hp * GLA_DV), lambda b, h, c: (rows(b, c), COL_GLA_V // (hp * GLA_DV) + h)),
            pl.BlockSpec((tb, hp * GLA_DV), lambda b, h, c: (rows(b, c), COL_GLA_G // (hp * GLA_DV) + h)),
            pl.BlockSpec((tb, LANE), lambda b, h, c: (rows(b, c), COL_GLA_A // LANE)),
            pl.BlockSpec((LANE, hp * GLA_DK), lambda b, h, c: (0, h)),
            pl.BlockSpec((1, hp * GLA_DK), lambda b, h, c: (0, h)),
            pl.BlockSpec((1, GLA_DV), fixed),
            pl.BlockSpec(prefix.shape, fixed),
            pl.BlockSpec(level.shape, fixed),
            pl.BlockSpec((None, hp, GLA_DV, GLA_DK), lambda b, h, c: (b, h, 0, 0)),
        ],
        out_specs=[
            pl.BlockSpec((tb, hp * GLA_DV), lambda b, h, c: (b * n_steps + c, h)),
            pl.BlockSpec((None, hp, GLA_DV, GLA_DK), lambda b, h, c: (b, h, 0, 0)),
        ],
        out_shape=[
            jax.ShapeDtypeStruct((batch * seq, W_GLA), BF16),
            jax.ShapeDtypeStruct((batch, GLA_HEADS, GLA_DV, GLA_DK), F32),
        ],
        scratch_shapes=[pltpu.VMEM((hp, GLA_DV, GLA_DK), F32)],
        compiler_params=_params(("parallel", "parallel", "arbitrary"), VMEM_LIMIT),
        name="gla",
    )(proj, proj, proj, proj, proj, wa_pad, ba, norm_g, jnp.asarray(prefix, BF16), jnp.asarray(level), s0_t)


def _sb_suffix_and_total():
    r2 = lax.broadcasted_iota(jnp.int32, (SB_BLOCK, 2 * SB_BLOCK), 0)
    c2 = lax.broadcasted_iota(jnp.int32, (SB_BLOCK, 2 * SB_BLOCK), 1)
    return jnp.where((r2 > c2) | (c2 >= SB_BLOCK), 1.0, 0.0).astype(BF16)


def _sb_causal(n_rows):
    row = lax.broadcasted_iota(jnp.int32, (n_rows, SB_BLOCK), 0)
    col = lax.broadcasted_iota(jnp.int32, (n_rows, SB_BLOCK), 1)
    return col < row


def _head_lanes(j):
    return slice(j * SB_HEAD, (j + 1) * SB_HEAD)


def _sb_blocks(qs, key_values, runs, accs, suffix_and_total, causal):
    tq = qs[0].shape[0]
    log_betas, parts = [], []
    for q, (kblk, _) in zip(qs, key_values):
        z = _dot_nt(q, kblk) * SB_SCALE
        sp = jnp.maximum(z, 0.0) + jnp.log1p(jnp.exp(-jnp.abs(z)))
        log_1m = -sp
        if causal is not None:
            log_1m = jnp.where(causal, log_1m, 0.0)
        log_betas.append(z - sp)
        parts.extend(_split_bf16(log_1m))
    sums = jnp.dot(jnp.concatenate(parts, axis=0), suffix_and_total, preferred_element_type=F32)
    new_runs, new_accs = [], []
    for j, (_, vblk) in enumerate(key_values):
        cs = sums[2 * j * tq:(2 * j + 1) * tq] + sums[(2 * j + 1) * tq:(2 * j + 2) * tq]
        att = jnp.exp(log_betas[j] + cs[:, :SB_BLOCK] + runs[j])
        if causal is not None:
            att = jnp.where(causal, att, 0.0)
        new_accs.append(accs[j] + jnp.dot(att.astype(BF16), vblk, preferred_element_type=F32))
        new_runs.append(runs[j] + cs[:, SB_BLOCK:])
    return new_runs, new_accs


def _sb_sweep(qs, key_value_block, n_blocks, runs, accs, suffix_and_total):
    nh = len(qs)

    def run_max(runs):
        return functools.reduce(jnp.maximum, [jnp.max(r) for r in runs])

    def cond(carry):
        i, _, _, rmax = carry
        return jnp.logical_and(i < n_blocks, rmax > EXP_ZERO_BELOW)

    def body(carry):
        i, runs, accs, _ = carry
        rows = pl.ds(pl.multiple_of((n_blocks - 1 - i) * SB_BLOCK, SB_BLOCK), SB_BLOCK)
        runs, accs = _sb_blocks(qs, [key_value_block(rows, j) for j in range(nh)], runs, accs,
                                suffix_and_total, None)
        return i + 1, tuple(runs), tuple(accs), run_max(runs)

    _, _, accs, _ = lax.while_loop(cond, body, (jnp.int32(0), tuple(runs), tuple(accs), run_max(runs)))
    return accs


def _sb_finish(accs, g_ref, o_ref):
    for j, acc in enumerate(accs):
        ms = jnp.mean(acc * acc, axis=-1, keepdims=True)
        o_ref[:, _head_lanes(j)] = (acc * lax.rsqrt(ms + NORM_EPS) * g_ref[:, _head_lanes(j)]).astype(o_ref.dtype)


def _sb_prompt_kernel(q_ref, k_ref, v_ref, g_ref, o_ref):
    qi = pl.program_id(2)
    suffix_and_total = _sb_suffix_and_total()
    causal = _sb_causal(SB_BLOCK)
    zero = jnp.zeros((SB_BLOCK, SB_BLOCK), F32)

    def key_value_block(rows, j):
        return k_ref[rows, _head_lanes(j)].astype(BF16), v_ref[rows, _head_lanes(j)].astype(BF16)

    diag = pl.ds(pl.multiple_of(qi * SB_BLOCK, SB_BLOCK), SB_BLOCK)
    heads = range(SB_HEADS_PER_STEP)
    qs = [q_ref[:, _head_lanes(j)].astype(BF16) for j in heads]
    runs, accs = _sb_blocks(qs, [key_value_block(diag, j) for j in heads], [zero] * len(heads), [zero] * len(heads),
                            suffix_and_total, causal)
    _sb_finish(_sb_sweep(qs, key_value_block, qi, runs, accs, suffix_and_total), g_ref, o_ref)


def _sb_prompt(proj, batch, seq, norm_g):
    nq = seq // SB_BLOCK
    wide = SB_HEADS_PER_STEP * SB_HEAD
    return pl.pallas_call(
        _sb_prompt_kernel,
        grid=(batch, SB_HEADS // SB_HEADS_PER_STEP, nq),
        in_specs=[
            pl.BlockSpec((SB_BLOCK, wide), lambda b, h, i: (b * nq + i, COL_SB_Q // wide + h)),
            pl.BlockSpec((seq, wide), lambda b, h, i: (b, COL_SB_K // wide + h)),
            pl.BlockSpec((seq, wide), lambda b, h, i: (b, COL_SB_V // wide + h)),
            pl.BlockSpec((1, wide), lambda b, h, i: (0, h)),
        ],
        out_specs=pl.BlockSpec((SB_BLOCK, wide), lambda b, h, i: (b * nq + i, h)),
        out_shape=jax.ShapeDtypeStruct((batch * seq, W_SB), BF16),
        compiler_params=_params(("parallel", "parallel", "arbitrary"), VMEM_LIMIT),
        name="sb_prompt",
    )(proj, proj, proj, norm_g)


def _sb_sample_kernel(q_ref, k_ref, v_ref, kp_ref, vp_ref, g_ref, o_ref, *, seq, n_past_blocks):
    suffix_and_total = _sb_suffix_and_total()
    causal = _sb_causal(seq)
    pad = jnp.zeros((SB_BLOCK - seq, SB_HEAD), BF16)
    zero = jnp.zeros((seq, SB_BLOCK), F32)

    def past_block(rows, j):
        return kp_ref[rows, j, :].astype(BF16), vp_ref[rows, j, :].astype(BF16)

    def new_block(j):
        return (jnp.concatenate([k_ref[:, _head_lanes(j)].astype(BF16), pad], axis=0),
                jnp.concatenate([v_ref[:, _head_lanes(j)].astype(BF16), pad], axis=0))

    heads = range(SB_HEADS)
    qs = [q_ref[:, _head_lanes(j)].astype(BF16) for j in heads]
    runs, accs = _sb_blocks(qs, [new_block(j) for j in heads], [zero] * len(heads), [zero] * len(heads),
                            suffix_and_total, causal)
    _sb_finish(_sb_sweep(qs, past_block, n_past_blocks, runs, accs, suffix_and_total), g_ref, o_ref)


def _sb_sample(proj, row0, batch, seq, layer, cache_k, cache_v, norm_g):
    past = cache_k.shape[2]
    rb0 = row0 // seq
    kern = functools.partial(_sb_sample_kernel, seq=seq, n_past_blocks=past // SB_BLOCK)
    cache_spec = pl.BlockSpec((None, None, past, SB_HEADS, SB_HEAD), lambda b: (layer, b, 0, 0, 0))
    return pl.pallas_call(
        kern,
        grid=(batch,),
        in_specs=[
            pl.BlockSpec((seq, W_SB), lambda b: (rb0 + b, COL_SB_Q // W_SB)),
            pl.BlockSpec((seq, W_SB), lambda b: (rb0 + b, COL_SB_K // W_SB)),
            pl.BlockSpec((seq, W_SB), lambda b: (rb0 + b, COL_SB_V // W_SB)),
            cache_spec, cache_spec,
            pl.BlockSpec((1, W_SB), lambda b: (0, 0)),
        ],
        out_specs=pl.BlockSpec((seq, W_SB), lambda b: (b, 0)),
        out_shape=jax.ShapeDtypeStruct((batch * seq, W_SB), BF16),
        compiler_params=_params(("parallel",), VMEM_LIMIT),
        name="sb_sample",
    )(proj, proj, proj, cache_k, cache_v, norm_g)


def _rwkv_prep_kernel(p_ref, bnd_ref, mu_ref, w0_ref, a0_ref, w2_ref, a2_ref, g2_ref,
                      r_out, w_out, k_out, v_out, a_out, g_out):
    p = p_ref[...]
    first = lax.broadcasted_iota(jnp.int32, p.shape, 0) == 0
    prev = jnp.where(first, bnd_ref[...], pltpu.roll(p, 1, 0))
    pm = p + (prev - p) * mu_ref[...]
    o = 0
    r = pm[:, o:o + W_RWKV]; o += W_RWKV
    w_low = pm[:, o:o + D_LORA]; o += D_LORA
    k = pm[:, o:o + W_RWKV]; o += W_RWKV
    v = pm[:, o:o + W_RWKV]; o += W_RWKV
    a_low = pm[:, o:o + D_LORA]; o += D_LORA
    g_low = pm[:, o:o + D_GATE_PAD]
    log_w = -math.exp(-0.5) * jax.nn.sigmoid(
        w0_ref[...] + jnp.dot(jnp.tanh(w_low).astype(BF16), w2_ref[...], preferred_element_type=F32))
    a = jax.nn.sigmoid(a0_ref[...] + jnp.dot(a_low.astype(BF16), a2_ref[...], preferred_element_type=F32))
    g = jnp.dot(jax.nn.sigmoid(g_low).astype(BF16), g2_ref[...], preferred_element_type=F32)
    r_out[...] = r
    w_out[...] = jnp.exp(log_w)
    k_out[...] = k
    v_out[...] = v
    a_out[...] = a
    g_out[...] = g


def _rwkv_prep(proj, row0, batch, seq, shift, mu, w0, a0, w2, a2, g2):
    tm = _tile(seq, (128, 64, 32, 16, 8))
    nt = seq // tm
    m = batch * seq
    rb0 = row0 // tm
    last = proj[row0 + tm - 1:row0 + m:tm, :P_RWKV_PAD].reshape(batch, nt, P_RWKV_PAD)
    bnd = jnp.concatenate([shift[:, None, :], last[:, :nt - 1]], axis=1).reshape(batch * nt, 1, P_RWKV_PAD)
    row = lambda i: (i, 0)
    fixed = lambda i: (0, 0)
    vec = pl.BlockSpec((1, W_RWKV), fixed)
    out = pl.BlockSpec((tm, W_RWKV), row)
    return pl.pallas_call(
        _rwkv_prep_kernel,
        grid=(m // tm,),
        in_specs=[pl.BlockSpec((tm, P_RWKV_PAD), lambda i: (rb0 + i, 0)),
                  pl.BlockSpec((None, 1, P_RWKV_PAD), lambda i: (i, 0, 0)),
                  pl.BlockSpec((1, P_RWKV_PAD), fixed), vec, vec,
                  pl.BlockSpec((D_LORA, W_RWKV), fixed), pl.BlockSpec((D_LORA, W_RWKV), fixed),
                  pl.BlockSpec((D_GATE_PAD, W_RWKV), fixed)],
        out_specs=[out] * 6,
        out_shape=[jax.ShapeDtypeStruct((m, W_RWKV), F32)] * 6,
        compiler_params=_params(("parallel",), VMEM_LIMIT),
        name="rwkv_prep",
    )(proj, bnd, mu, w0, a0, w2, a2, g2)


def _spread_lanes(copies):
    lanes = copies[0].shape[-1]
    full = lanes // LANE
    pieces = [c[..., i * LANE:(i + 1) * LANE] for i in range(full) for c in copies]
    if lanes % LANE:
        pieces += [c[..., full * LANE:] for c in copies]
    return pieces[0] if len(pieces) == 1 else jnp.concatenate(pieces, axis=-1)


def _gather_lanes(x, n_copies):
    lanes = x.shape[-1] // n_copies
    full, rem = lanes // LANE, lanes % LANE
    out = []
    for g in range(n_copies):
        pieces = [x[..., (i * n_copies + g) * LANE:(i * n_copies + g + 1) * LANE] for i in range(full)]
        if rem:
            start = full * n_copies * LANE + g * rem
            pieces.append(x[..., start:start + rem])
        out.append(pieces[0] if len(pieces) == 1 else jnp.concatenate(pieces, axis=-1))
    return out


def _rwkv_scan_kernel(r_ref, w_ref, k_ref, v_ref, a_ref, rk_ref, lng_ref, lnb_ref, kkp_ref, kap_ref, s0_ref,
                      z_ref, sT_ref, s_scr, kk_scr, b_scr, w_scr, k_scr, r_scr, v_scr, y_scr,
                      *, t_block, n_steps, vs):
    c = pl.program_id(1)
    n = RWKV_HEAD
    nv = n // vs
    lanes_in = r_ref.shape[-1]
    lanes_work = lanes_in * vs

    def dup(x):
        return _spread_lanes([x] * vs)

    @pl.when(c == 0)
    def _():
        s_scr[...] = s0_ref[...]

    def step(t, carry):
        k_raw = k_ref[t]
        a = a_ref[t]
        kk_raw = k_raw * kkp_ref[...]
        norm = jnp.sqrt(jnp.sum(kk_raw * kk_raw, axis=0, keepdims=True))
        kk = kk_raw / jnp.maximum(norm, 1e-12)
        k_t = k_raw * (1.0 + (a - 1.0) * kap_ref[...])
        kk_scr[...] = dup(kk)
        b_scr[...] = dup(kk * a)
        w_scr[...] = dup(w_ref[t])
        k_scr[...] = dup(k_t)
        r_scr[...] = dup(r_ref[t])
        v_t = v_ref[t]
        v_scr[...] = _spread_lanes([v_t[g * nv:(g + 1) * nv] for g in range(vs)])

        for lg in range(lanes_work // LANE):
            ls = pl.ds(lg * LANE, LANE)

            def value_rows(vo, carry2):
                v0 = pl.multiple_of(vo * SUBLANE, SUBLANE)
                v_tile = v_scr[pl.ds(v0, SUBLANE), ls]
                y_rows = []
                for vi in range(SUBLANE):
                    s_v = s_scr[v0 + vi, :, ls]
                    sa = jnp.sum(s_v * kk_scr[:, ls], axis=0, keepdims=True)
                    s_new = s_v * w_scr[:, ls] - sa * b_scr[:, ls] + v_tile[vi:vi + 1, :] * k_scr[:, ls]
                    s_scr[v0 + vi, :, ls] = s_new
                    y_rows.append(jnp.sum(s_new * r_scr[:, ls], axis=0, keepdims=True))
                y_scr[pl.ds(v0, SUBLANE), ls] = jnp.concatenate(y_rows, axis=0)
                return carry2

            lax.fori_loop(0, nv // SUBLANE, value_rows, 0)

        y_groups = _gather_lanes(y_scr[...], vs)
        y = y_groups[0] if vs == 1 else jnp.concatenate(y_groups, axis=0)
        mean = jnp.mean(y, axis=0, keepdims=True)
        d = y - mean
        var = jnp.mean(d * d, axis=0, keepdims=True)
        bonus = jnp.sum(r_ref[t] * k_t * rk_ref[...], axis=0, keepdims=True) * v_t
        z_ref[t] = d * lax.rsqrt(var + RWKV_GN_EPS) * lng_ref[...] + lnb_ref[...] + bonus
        return carry

    lax.fori_loop(0, t_block, step, 0)

    @pl.when(c == n_steps - 1)
    def _():
        sT_ref[...] = s_scr[...]


def _rwkv_scan(r, w, k, v, a, rk, lng, lnb, kkp, kap, s0, vs):
    seq, n, lanes = r.shape
    nv = n // vs
    t_block = _tile(seq, (8, 4, 2, 1))
    n_steps = seq // t_block
    lb = lanes if vs > 1 else _tile(lanes, (3 * LANE, 2 * LANE, LANE))
    lbw = lb * vs
    seq_spec = pl.BlockSpec((t_block, n, lb), lambda g, c: (c, 0, g))
    vec_spec = pl.BlockSpec((n, lb), lambda g, c: (0, g))
    st_spec = pl.BlockSpec((nv, n, lbw), lambda g, c: (0, 0, g))
    kern = functools.partial(_rwkv_scan_kernel, t_block=t_block, n_steps=n_steps, vs=vs)
    work = lambda rows: pltpu.VMEM((rows, lbw), F32)
    return pl.pallas_call(
        kern,
        grid=(lanes // lb, n_steps),
        in_specs=[seq_spec] * 5 + [vec_spec] * 5 + [st_spec],
        out_specs=[seq_spec, st_spec],
        out_shape=[jax.ShapeDtypeStruct((seq, n, lanes), F32), jax.ShapeDtypeStruct((nv, n, lanes * vs), F32)],
        scratch_shapes=[pltpu.VMEM((nv, n, lbw), F32), work(n), work(n), work(n), work(n), work(n),
                        work(nv), work(nv)],
        compiler_params=_params(("parallel", "arbitrary"), VMEM_LIMIT),
        name="rwkv_scan",
    )(r, w, k, v, a, rk, lng, lnb, kkp, kap, s0)


def _merge_kernel(gla_p, z_p, g_p, sb_p, gla_s, z_s, g_s, sb_s, o_ref, *, n_prompt_tiles):
    def write(gla_ref, z_ref, g_ref, sb_ref):
        o_ref[:, 0:W_GLA] = gla_ref[...]
        o_ref[:, W_GLA:W_GLA + W_RWKV] = (z_ref[...] * g_ref[...]).astype(o_ref.dtype)
        o_ref[:, W_GLA + W_RWKV:D_MODEL] = sb_ref[...]

    i = pl.program_id(0)

    @pl.when(i < n_prompt_tiles)
    def _():
        write(gla_p, z_p, g_p, sb_p)

    @pl.when(i >= n_prompt_tiles)
    def _():
        write(gla_s, z_s, g_s, sb_s)


def _merge(prompt, sample):
    mp, ms = prompt[1].shape[0], sample[1].shape[0]
    tm = _tile(math.gcd(mp, ms), (512, 256, 128, 64, 32, 16))
    n_p, n_s = mp // tm, ms // tm
    widths = (W_GLA, W_RWKV, W_RWKV, W_SB)
    p_specs = [pl.BlockSpec((tm, wd), lambda i: (jnp.minimum(i, n_p - 1), 0)) for wd in widths]
    s_specs = [pl.BlockSpec((tm, wd), lambda i: (jnp.maximum(i - n_p, 0), 0)) for wd in widths]
    return pl.pallas_call(
        functools.partial(_merge_kernel, n_prompt_tiles=n_p),
        grid=(n_p + n_s,),
        in_specs=p_specs + s_specs,
        out_specs=pl.BlockSpec((tm, D_MODEL), lambda i: (i, 0)),
        out_shape=jax.ShapeDtypeStruct((mp + ms, D_MODEL), BF16),
        compiler_params=_params(("parallel",), VMEM_LIMIT),
        name="merge",
    )(*prompt, *sample)


def _to_scan_layout(x, batch, seq):
    x = x.reshape(batch, seq, RWKV_HEADS, RWKV_HEAD).transpose(1, 3, 0, 2)
    return x.reshape(seq, RWKV_HEAD, batch * RWKV_HEADS)


def _from_scan_layout(z, batch, seq):
    z = z.reshape(seq, RWKV_HEAD, batch, RWKV_HEADS)
    return z.transpose(2, 0, 3, 1).reshape(batch * seq, W_RWKV)


def _head_vec_to_scan_layout(x, batch):
    x = x.reshape(RWKV_HEADS, RWKV_HEAD).T
    return jnp.tile(x[:, None, :], (1, batch, 1)).reshape(RWKV_HEAD, batch * RWKV_HEADS)


def _rwkv_group(seqs, batch, seq, params, s0):
    lanes = batch * RWKV_HEADS
    n = RWKV_HEAD
    vs = next(f for f in (1, 2, 4, 8) if (lanes * f) % LANE == 0)
    nv = n // vs
    s0_l = s0.transpose(2, 3, 0, 1).reshape(n, n, lanes)
    s0_l = _spread_lanes([s0_l[g * nv:(g + 1) * nv] for g in range(vs)])
    ins = [_to_scan_layout(x, batch, seq) for x in seqs]
    vecs = [_head_vec_to_scan_layout(x, batch) for x in params]
    z, s_t = _rwkv_scan(*ins, *vecs, s0_l, vs)
    s_new = jnp.concatenate(_gather_lanes(s_t, vs), axis=0).reshape(n, n, batch, RWKV_HEADS).transpose(2, 3, 0, 1)
    return _from_scan_layout(z, batch, seq), s_new


W_IN_SEGMENTS = (
    (COL_RWKV, P_GLA, P_RWKV, P_RWKV_PAD - P_RWKV),
    (COL_GLA_Q, 0, GLA_QK, 0),
    (COL_SB_Q, P_GLA + P_RWKV, P_SB, 0),
    (COL_GLA_V, 2 * GLA_QK, W_GLA, 0),
    (COL_GLA_G, 2 * GLA_QK + W_GLA, W_GLA, 0),
    (COL_GLA_K, GLA_QK, GLA_QK, 0),
    (COL_GLA_A, 2 * GLA_QK + 2 * W_GLA, GLA_LOWRANK, P_PAD - COL_GLA_A - GLA_LOWRANK),
)


def _rearrange_w_in_kernel(w_ref, o_ref):
    rows = w_ref.shape[0]
    for dst, src, width, pad in W_IN_SEGMENTS:
        piece = w_ref[:, src:src + width].astype(BF16)
        if pad:
            piece = jnp.concatenate([piece, jnp.zeros((rows, pad), BF16)], axis=1)
        o_ref[:, dst:dst + width + pad] = piece


def _rearrange_w_in(w_in, layer):
    _, d, p = w_in.shape
    tk = _tile(d, (64, 32, 16))
    return pl.pallas_call(
        _rearrange_w_in_kernel,
        grid=(d // tk,),
        in_specs=[pl.BlockSpec((None, tk, p), lambda i: (layer, i, 0))],
        out_specs=pl.BlockSpec((tk, P_PAD), lambda i: (i, 0)),
        out_shape=jax.ShapeDtypeStruct((d, P_PAD), BF16),
        compiler_params=_params(("parallel",), VMEM_LIMIT),
        name="rearrange_w_in",
    )(w_in)


def kernel(x_prompt, x_sample, state_gla, state_rwkv, state_rwkv_shift, cache_sb_k, cache_sb_v, norm1_g, w_in, gla_wa2, gla_ba, gla_norm_g, rwkv_mu, rwkv_w0, rwkv_w2, rwkv_a0, rwkv_a2, rwkv_g2, rwkv_k_k, rwkv_k_a, rwkv_r_k, rwkv_ln_g, rwkv_ln_b, sb_norm_g, w_out, norm2_g, w_gate, w_up, w_down, final_g):
    bp, tp, d = x_prompt.shape
    bs, ts, _ = x_sample.shape
    depth = w_in.shape[0]
    mp, ms = bp * tp, bs * ts
    x = jnp.concatenate([x_prompt.reshape(mp, d), x_sample.reshape(ms, d)], axis=0)

    outs = {name: [] for name in ("gla_p", "gla_s", "rw_p", "rw_s", "sh_p", "sh_s", "k_p", "k_s", "v_p", "v_s")}
    row1 = lambda a: a.reshape(1, -1)
    for l in range(depth):
        h = _rmsnorm(x, norm1_g[l], BF16)
        proj = _matmul(h, _rearrange_w_in(w_in, l), F32)
        outs["sh_p"].append(proj[tp - 1:mp:tp, :P_RWKV].reshape(bp, 1, P_RWKV))
        outs["sh_s"].append(proj[mp + ts - 1::ts, :P_RWKV].reshape(bs, 1, P_RWKV))
        for nm, c0 in (("k", COL_SB_K), ("v", COL_SB_V)):
            outs[nm + "_p"].append(proj[:mp, c0:c0 + W_SB].reshape(bp, tp, SB_HEADS, SB_HEAD))
            outs[nm + "_s"].append(proj[mp:, c0:c0 + W_SB].reshape(bs, ts, SB_HEADS, SB_HEAD))

        wa_pad = jnp.pad(gla_wa2[l], ((0, LANE - GLA_LOWRANK), (0, 0))).astype(BF16)
        gla_args = (wa_pad, row1(gla_ba[l]), row1(gla_norm_g[l]))
        o_gla_p, sg_p = _gla(proj, 0, bp, tp, *gla_args, jnp.zeros((bp, GLA_HEADS, GLA_DV, GLA_DK), F32))
        o_gla_s, sg_s = _gla(proj, mp, bs, ts, *gla_args, state_gla[l].swapaxes(-1, -2))
        outs["gla_p"].append(sg_p.swapaxes(-1, -2))
        outs["gla_s"].append(sg_s.swapaxes(-1, -2))

        lane_pad = (0, P_RWKV_PAD - P_RWKV)
        prep_args = (row1(jnp.pad(rwkv_mu[l], lane_pad)), row1(rwkv_w0[l]), row1(rwkv_a0[l]),
                     rwkv_w2[l].astype(BF16), rwkv_a2[l].astype(BF16),
                     jnp.pad(rwkv_g2[l], ((0, D_GATE_PAD - D_GATE_LORA), (0, 0))).astype(BF16))
        vec_args = (rwkv_r_k[l].reshape(-1), rwkv_ln_g[l], rwkv_ln_b[l], rwkv_k_k[l], rwkv_k_a[l])
        *seqs_p, gate_p = _rwkv_prep(proj, 0, bp, tp, jnp.zeros((bp, P_RWKV_PAD), F32), *prep_args)
        *seqs_s, gate_s = _rwkv_prep(proj, mp, bs, ts, jnp.pad(state_rwkv_shift[l][:, 0, :], ((0, 0), lane_pad)),
                                     *prep_args)
        z_p, sr_p = _rwkv_group(seqs_p, bp, tp, vec_args, jnp.zeros((bp, RWKV_HEADS, RWKV_HEAD, RWKV_HEAD), F32))
        z_s, sr_s = _rwkv_group(seqs_s, bs, ts, vec_args, state_rwkv[l])
        outs["rw_p"].append(sr_p)
        outs["rw_s"].append(sr_s)

        sbg = row1(sb_norm_g[l])
        o_sb_p = _sb_prompt(proj, bp, tp, sbg)
        o_sb_s = _sb_sample(proj, mp, bs, ts, l, cache_sb_k, cache_sb_v, sbg)

        merged = _merge((o_gla_p, z_p, gate_p, o_sb_p), (o_gla_s, z_s, gate_s, o_sb_s))
        x = _matmul_residual(merged, w_out[l].astype(BF16), x, tk=d)

        h2 = _rmsnorm(x, norm2_g[l], BF16)
        ff_pad = ((0, 0), (0, D_FF_PAD - D_FF))
        act = _swiglu(h2, jnp.pad(w_gate[l].astype(BF16), ff_pad), jnp.pad(w_up[l].astype(BF16), ff_pad))
        w_dn = jnp.pad(w_down[l].astype(BF16), ((0, D_FF_PAD - D_FF), (0, 0)))
        x = _matmul_residual(act, w_dn, x, tk=D_FF_PAD // 4)

    y_p, y_s = _rmsnorm_split(x, final_g, mp)
    sd = state_gla.dtype
    st = lambda name, dt=None: jnp.stack(outs[name]) if dt is None else jnp.stack(outs[name]).astype(dt)
    return (y_p.reshape(bp, tp, d), y_s.reshape(bs, ts, d),
            st("gla_p", sd), st("gla_s", sd), st("rw_p", sd), st("rw_s", sd),
            st("sh_p"), st("sh_s"), st("k_p"), st("k_s"), st("v_p"), st("v_s"))
```

```python
import functools
import math

import jax
import jax.numpy as jnp
import numpy as np
from jax import lax
from jax.experimental import pallas as pl
from jax.experimental.pallas import tpu as pltpu

F32 = jnp.float32
BF16 = jnp.bfloat16

D_MODEL = 4096
NORM_EPS = 1e-5
W_GLA = 1536
W_RWKV = 1536
W_SB = 1024
GLA_HEADS = 6
GLA_DV = 256
GLA_DK = 128
GLA_QK = 768
GLA_LOWRANK = 16
GLA_TAU = 16.0
GLA_CHUNK = 64
GLA_HEADS_PER_STEP = 3
RWKV_HEAD = 64
RWKV_HEADS = 24
D_LORA = 128
D_GATE_LORA = 480
D_GATE_PAD = 512
RWKV_GN_EPS = 64e-5
SB_HEAD = 128
SB_HEADS = 8
SB_SCALE = SB_HEAD ** -0.5
SB_BLOCK = 128
SB_HEADS_PER_STEP = 8
D_FF = 11008
D_FF_PAD = 11264
P_GLA = 4624
P_RWKV = 5344
P_RWKV_PAD = 5376
P_SB = 3072

COL_RWKV = 0
COL_GLA_Q = P_RWKV_PAD
COL_SB_Q = COL_GLA_Q + GLA_QK
COL_SB_K = COL_SB_Q + W_SB
COL_SB_V = COL_SB_K + W_SB
COL_GLA_V = COL_SB_V + W_SB
COL_GLA_G = COL_GLA_V + W_GLA
COL_GLA_K = COL_GLA_G + W_GLA
COL_GLA_A = COL_GLA_K + GLA_QK
P_PAD = 13312

LANE = 128
SUBLANE = 8
VMEM_LIMIT = 56 * 1024 * 1024

EXP_ZERO_BELOW = -104.0


def _tile(n, candidates):
    for c in candidates:
        if n % c == 0:
            return c
    return n


def _params(sem, vmem=None):
    return pltpu.CompilerParams(dimension_semantics=sem, vmem_limit_bytes=vmem)


def _rmsnorm_kernel(x_ref, g_ref, o_ref):
    x = x_ref[...]
    ms = jnp.mean(x * x, axis=-1, keepdims=True)
    o_ref[...] = (x * lax.rsqrt(ms + NORM_EPS) * g_ref[...]).astype(o_ref.dtype)


def _rmsnorm(x, g, out_dtype):
    m, d = x.shape
    tm = _tile(m, (512, 256, 128, 64, 32, 16, 8))
    return pl.pallas_call(
        _rmsnorm_kernel,
        grid=(m // tm,),
        in_specs=[pl.BlockSpec((tm, d), lambda i: (i, 0)), pl.BlockSpec((1, d), lambda i: (0, 0))],
        out_specs=pl.BlockSpec((tm, d), lambda i: (i, 0)),
        out_shape=jax.ShapeDtypeStruct((m, d), out_dtype),
        compiler_params=_params(("parallel",), VMEM_LIMIT),
        name="rmsnorm",
    )(x, g.reshape(1, d))


def _rmsnorm_split_kernel(x_ref, g_ref, a_ref, b_ref, *, n_first):
    x = x_ref[...]
    ms = jnp.mean(x * x, axis=-1, keepdims=True)
    y = x * lax.rsqrt(ms + NORM_EPS) * g_ref[...]
    i = pl.program_id(0)

    @pl.when(i < n_first)
    def _():
        a_ref[...] = y

    @pl.when(i >= n_first)
    def _():
        b_ref[...] = y


def _rmsnorm_split(x, g, m_first):
    m, d = x.shape
    tm = _tile(math.gcd(m_first, m - m_first), (512, 256, 128, 64, 32, 16, 8))
    n_first = m_first // tm
    return pl.pallas_call(
        functools.partial(_rmsnorm_split_kernel, n_first=n_first),
        grid=(m // tm,),
        in_specs=[pl.BlockSpec((tm, d), lambda i: (i, 0)), pl.BlockSpec((1, d), lambda i: (0, 0))],
        out_specs=[pl.BlockSpec((tm, d), lambda i: (jnp.minimum(i, n_first - 1), 0)),
                   pl.BlockSpec((tm, d), lambda i: (jnp.maximum(i - n_first, 0), 0))],
        out_shape=[jax.ShapeDtypeStruct((m_first, d), F32), jax.ShapeDtypeStruct((m - m_first, d), F32)],
        compiler_params=_params(("arbitrary",), VMEM_LIMIT),
        name="rmsnorm_split",
    )(x, g.reshape(1, d))


def _mm_kernel(x_ref, w_ref, o_ref):
    o_ref[...] = jnp.dot(x_ref[...], w_ref[...], preferred_element_type=F32).astype(o_ref.dtype)


def _mm_res_kernel(x_ref, w_ref, r_ref, o_ref):
    acc = jnp.dot(x_ref[...], w_ref[...], preferred_element_type=F32)
    k = pl.program_id(2)

    @pl.when(k == 0)
    def _():
        o_ref[...] = r_ref[...] + acc

    @pl.when(k != 0)
    def _():
        o_ref[...] += acc


def _swiglu_kernel(x_ref, wg_ref, wu_ref, o_ref):
    x = x_ref[...]
    g = jnp.dot(x, wg_ref[...], preferred_element_type=F32)
    u = jnp.dot(x, wu_ref[...], preferred_element_type=F32)
    o_ref[...] = (g * jax.nn.sigmoid(g) * u).astype(o_ref.dtype)


def _matmul(x, w, out_dtype):
    m, k = x.shape
    n = w.shape[1]
    tm = _tile(m, (1024, 512, 256, 128, 64, 32, 16, 8))
    tn = _tile(n, (1024, 512, 256, 128))
    return pl.pallas_call(
        _mm_kernel,
        grid=(m // tm, n // tn),
        in_specs=[pl.BlockSpec((tm, k), lambda i, j: (i, 0)), pl.BlockSpec((k, tn), lambda i, j: (0, j))],
        out_specs=pl.BlockSpec((tm, tn), lambda i, j: (i, j)),
        out_shape=jax.ShapeDtypeStruct((m, n), out_dtype),
        compiler_params=_params(("parallel", "parallel"), VMEM_LIMIT),
        name="matmul",
    )(x, w)


def _matmul_residual(x, w, res, tk):
    m, k = x.shape
    n = w.shape[1]
    tm = _tile(m, (1024, 512, 256, 128, 64, 32, 16, 8))
    tn = _tile(n, (1024, 512, 256, 128))
    return pl.pallas_call(
        _mm_res_kernel,
        grid=(m // tm, n // tn, k // tk),
        in_specs=[pl.BlockSpec((tm, tk), lambda i, j, kk: (i, kk)),
                  pl.BlockSpec((tk, tn), lambda i, j, kk: (kk, j)),
                  pl.BlockSpec((tm, tn), lambda i, j, kk: (i, j))],
        out_specs=pl.BlockSpec((tm, tn), lambda i, j, kk: (i, j)),
        out_shape=jax.ShapeDtypeStruct((m, n), F32),
        compiler_params=_params(("parallel", "parallel", "arbitrary"), VMEM_LIMIT),
        name="matmul_residual",
    )(x, w, res)


def _swiglu(x, wg, wu):
    m, k = x.shape
    n = wg.shape[1]
    tm = _tile(m, (1024, 512, 256, 128, 64, 32, 16, 8))
    tn = _tile(n, (512, 256, 128))
    return pl.pallas_call(
        _swiglu_kernel,
        grid=(m // tm, n // tn),
        in_specs=[pl.BlockSpec((tm, k), lambda i, j: (i, 0)),
                  pl.BlockSpec((k, tn), lambda i, j: (0, j)),
                  pl.BlockSpec((k, tn), lambda i, j: (0, j))],
        out_specs=pl.BlockSpec((tm, tn), lambda i, j: (i, j)),
        out_shape=jax.ShapeDtypeStruct((m, n), BF16),
        compiler_params=_params(("parallel", "parallel"), VMEM_LIMIT),
        name="swiglu",
    )(x, wg, wu)


def _log_sigmoid(x):
    return jnp.minimum(x, 0.0) - jnp.log1p(jnp.exp(-jnp.abs(x)))


def _split_bf16(x):
    hi = x.astype(BF16)
    lo = (x - hi.astype(F32)).astype(BF16)
    return hi, lo


def _dot_nt(a, b):
    return lax.dot_general(a, b, (((1,), (1,)), ((), ())), preferred_element_type=F32)


def _dot_tn(a, b):
    return lax.dot_general(a, b, (((0,), (0,)), ((), ())), preferred_element_type=F32)


def _gla_tables(chunk):
    levels = chunk.bit_length() - 1
    idx = np.arange(chunk)
    prefix = np.zeros((1 + 2 * levels, chunk, chunk), np.float32)
    prefix[0] = idx[None, :] <= idx[:, None]
    for lv in range(levels):
        m = 1 << lv
        mid = (idx // (2 * m)) * 2 * m + m
        for t in range(chunk):
            if t >= mid[t]:
                prefix[1 + lv, t, mid[t]:t + 1] = 1.0
            else:
                prefix[1 + levels + lv, t, t + 1:mid[t]] = 1.0
    x = idx[:, None] ^ idx[None, :]
    level = np.where(idx[:, None] == idx[None, :], levels, np.floor(np.log2(np.maximum(x, 1))).astype(np.int32))
    level = np.where(idx[None, :] > idx[:, None], -1, level).astype(np.int32)
    return prefix.reshape(-1, chunk), level, levels


def _gla_kernel(q_ref, k_ref, v_ref, g_ref, al_ref, wa_ref, ba_ref, ng_ref, pre_ref, lvl_ref, s0_ref,
                o_ref, sT_ref, s_scr, *, chunk, levels, n_chunks, n_steps):
    c = pl.program_id(2)

    @pl.when(c == 0)
    def _():
        s_scr[...] = s0_ref[...]

    prefix = pre_ref[...]
    level = lvl_ref[...]
    ng = ng_ref[...]
    x = jnp.dot(al_ref[...].astype(BF16), wa_ref[...], preferred_element_type=F32) + ba_ref[...]
    hi, lo = _split_bf16(_log_sigmoid(x) * (1.0 / GLA_TAU))
    width = GLA_HEADS_PER_STEP * GLA_DK

    for ci in range(n_chunks):
        rows = pl.ds(ci * chunk, chunk)
        r0 = ci * chunk
        parts = jnp.dot(prefix, jnp.concatenate([hi[r0:r0 + chunk], lo[r0:r0 + chunk]], axis=1),
                        preferred_element_type=F32)
        sums_all = parts[:, :width] + parts[:, width:]
        for j in range(GLA_HEADS_PER_STEP):
            kl = slice(j * GLA_DK, (j + 1) * GLA_DK)
            vl = slice(j * GLA_DV, (j + 1) * GLA_DV)
            sums = sums_all[:, kl]
            b = sums[:chunk]
            b_last = b[chunk - 1:chunk, :]
            decay = jnp.exp(sums[chunk:])
            q = q_ref[rows, kl] * (GLA_DK ** -0.5)
            k = k_ref[rows, kl]
            v = v_ref[rows, vl].astype(BF16)
            scores = jnp.where(level == levels, _dot_nt(q.astype(BF16), k.astype(BF16)), 0.0)
            for lv in range(levels):
                q_lv = (q * decay[lv * chunk:(lv + 1) * chunk]).astype(BF16)
                k_lv = (k * decay[(levels + lv) * chunk:(levels + lv + 1) * chunk]).astype(BF16)
                scores = jnp.where(level == lv, _dot_nt(q_lv, k_lv), scores)
            qd = (q * jnp.exp(b)).astype(BF16)
            s_t = s_scr[j]
            o = jnp.dot(scores.astype(BF16), v, preferred_element_type=F32) + _dot_nt(qd, s_t.astype(BF16))
            kdec = (k * jnp.exp(b_last - b)).astype(BF16)
            s_scr[j] = s_t * jnp.exp(b_last) + _dot_tn(v, kdec)
            ms = jnp.mean(o * o, axis=-1, keepdims=True)
            gg = g_ref[rows, vl]
            o_ref[rows, vl] = (o * lax.rsqrt(ms + NORM_EPS) * ng * (gg * jax.nn.sigmoid(gg))).astype(o_ref.dtype)

    @pl.when(c == n_steps - 1)
    def _():
        sT_ref[...] = s_scr[...]


def _gla(proj, row0, batch, seq, wa_pad, ba, norm_g, s0_t):
    chunk = min(GLA_CHUNK, seq)
    tb = _tile(seq, (256, 128, 64, 32, 16, 8))
    n_chunks = tb // chunk
    n_steps = seq // tb
    rb0 = row0 // tb
    hp = GLA_HEADS_PER_STEP
    prefix, level, levels = _gla_tables(chunk)

    def rows(b, c):
        return rb0 + b * n_steps + c

    fixed = lambda b, h, c: (0, 0)
    kern = functools.partial(_gla_kernel, chunk=chunk, levels=levels, n_chunks=n_chunks, n_steps=n_steps)
    return pl.pallas_call(
        kern,
        grid=(batch, GLA_HEADS // hp, n_steps),
        in_specs=[
            pl.BlockSpec((tb, hp * GLA_DK), lambda b, h, c: (rows(b, c), COL_GLA_Q // (hp * GLA_DK) + h)),
            pl.BlockSpec((tb, hp * GLA_DK), lambda b, h, c: (rows(b, c), COL_GLA_K // (hp * GLA_DK) + h)),
            pl.BlockSpec((tb, hp * GLA_DV), lambda b, h, c: (rows(b, c), COL_GLA_V // (hp * GLA_DV) + h)),
            pl.BlockSpec((tb, hp * GLA_DV), lambda b, h, c: (rows(b, c), COL_GLA_G // (hp * GLA_DV) + h)),
            pl.BlockSpec((tb, LANE), lambda b, h, c: (rows(b, c), COL_GLA_A // LANE)),
            pl.BlockSpec((LANE, hp * GLA_DK), lambda b, h, c: (0, h)),
            pl.BlockSpec((1, hp * GLA_DK), lambda b, h, c: (0, h)),
            pl.BlockSpec((1, GLA_DV), fixed),
            pl.BlockSpec(prefix.shape, fixed),
            pl.BlockSpec(level.shape, fixed),
            pl.BlockSpec((None, hp, GLA_DV, GLA_DK), lambda b, h, c: (b, h, 0, 0)),
        ],
        out_specs=[
            pl.BlockSpec((tb, hp * GLA_DV), lambda b, h, c: (b * n_steps + c, h)),
            pl.BlockSpec((None, hp, GLA_DV, GLA_DK), lambda b, h, c: (b, h, 0, 0)),
        ],
        out_shape=[
            jax.ShapeDtypeStruct((batch * seq, W_GLA), BF16),
            jax.ShapeDtypeStruct((batch, GLA_HEADS, GLA_DV, GLA_DK), F32),
        ],
        scratch_shapes=[pltpu.VMEM((hp, GLA_DV, GLA_DK), F32)],
        compiler_params=_params(("parallel", "parallel", "arbitrary"), VMEM_LIMIT),
        name="gla",
    )(proj, proj, proj, proj, proj, wa_pad, ba, norm_g, jnp.asarray(prefix, BF16), jnp.asarray(level), s0_t)


def _sb_suffix_and_total():
    r2 = lax.broadcasted_iota(jnp.int32, (SB_BLOCK, 2 * SB_BLOCK), 0)
    c2 = lax.broadcasted_iota(jnp.int32, (SB_BLOCK, 2 * SB_BLOCK), 1)
    return jnp.where((r2 > c2) | (c2 >= SB_BLOCK), 1.0, 0.0).astype(BF16)


def _sb_causal(n_rows):
    row = lax.broadcasted_iota(jnp.int32, (n_rows, SB_BLOCK), 0)
    col = lax.broadcasted_iota(jnp.int32, (n_rows, SB_BLOCK), 1)
    return col < row


def _head_lanes(j):
    return slice(j * SB_HEAD, (j + 1) * SB_HEAD)


def _sb_blocks(qs, key_values, runs, accs, suffix_and_total, causal):
    tq = qs[0].shape[0]
    log_betas, parts = [], []
    for q, (kblk, _) in zip(qs, key_values):
        z = _dot_nt(q, kblk) * SB_SCALE
        sp = jnp.maximum(z, 0.0) + jnp.log1p(jnp.exp(-jnp.abs(z)))
        log_1m = -sp
        if causal is not None:
            log_1m = jnp.where(causal, log_1m, 0.0)
        log_betas.append(z - sp)
        parts.extend(_split_bf16(log_1m))
    sums = jnp.dot(jnp.concatenate(parts, axis=0), suffix_and_total, preferred_element_type=F32)
    new_runs, new_accs = [], []
    for j, (_, vblk) in enumerate(key_values):
        cs = sums[2 * j * tq:(2 * j + 1) * tq] + sums[(2 * j + 1) * tq:(2 * j + 2) * tq]
        att = jnp.exp(log_betas[j] + cs[:, :SB_BLOCK] + runs[j])
        if causal is not None:
            att = jnp.where(causal, att, 0.0)
        new_accs.append(accs[j] + jnp.dot(att.astype(BF16), vblk, preferred_element_type=F32))
        new_runs.append(runs[j] + cs[:, SB_BLOCK:])
    return new_runs, new_accs


def _sb_sweep(qs, key_value_block, n_blocks, runs, accs, suffix_and_total):
    nh = len(qs)

    def run_max(runs):
        return functools.reduce(jnp.maximum, [jnp.max(r) for r in runs])

    def cond(carry):
        i, _, _, rmax = carry
        return jnp.logical_and(i < n_blocks, rmax > EXP_ZERO_BELOW)

    def body(carry):
        i, runs, accs, _ = carry
        rows = pl.ds(pl.multiple_of((n_blocks - 1 - i) * SB_BLOCK, SB_BLOCK), SB_BLOCK)
        runs, accs = _sb_blocks(qs, [key_value_block(rows, j) for j in range(nh)], runs, accs,
                                suffix_and_total, None)
        return i + 1, tuple(runs), tuple(accs), run_max(runs)

    _, _, accs, _ = lax.while_loop(cond, body, (jnp.int32(0), tuple(runs), tuple(accs), run_max(runs)))
    return accs


def _sb_finish(accs, g_ref, o_ref):
    for j, acc in enumerate(accs):
        ms = jnp.mean(acc * acc, axis=-1, keepdims=True)
        o_ref[:, _head_lanes(j)] = (acc * lax.rsqrt(ms + NORM_EPS) * g_ref[:, _head_lanes(j)]).astype(o_ref.dtype)


def _sb_prompt_kernel(q_ref, k_ref, v_ref, g_ref, o_ref):
    qi = pl.program_id(2)
    suffix_and_total = _sb_suffix_and_total()
    causal = _sb_causal(SB_BLOCK)
    zero = jnp.zeros((SB_BLOCK, SB_BLOCK), F32)

    def key_value_block(rows, j):
        return k_ref[rows, _head_lanes(j)].astype(BF16), v_ref[rows, _head_lanes(j)].astype(BF16)

    diag = pl.ds(pl.multiple_of(qi * SB_BLOCK, SB_BLOCK), SB_BLOCK)
    heads = range(SB_HEADS_PER_STEP)
    qs = [q_ref[:, _head_lanes(j)].astype(BF16) for j in heads]
    runs, accs = _sb_blocks(qs, [key_value_block(diag, j) for j in heads], [zero] * len(heads), [zero] * len(heads),
                            suffix_and_total, causal)
    _sb_finish(_sb_sweep(qs, key_value_block, qi, runs, accs, suffix_and_total), g_ref, o_ref)


def _sb_prompt(proj, batch, seq, norm_g):
    nq = seq // SB_BLOCK
    wide = SB_HEADS_PER_STEP * SB_HEAD
    return pl.pallas_call(
        _sb_prompt_kernel,
        grid=(batch, SB_HEADS // SB_HEADS_PER_STEP, nq),
        in_specs=[
            pl.BlockSpec((SB_BLOCK, wide), lambda b, h, i: (b * nq + i, COL_SB_Q // wide + h)),
            pl.BlockSpec((seq, wide), lambda b, h, i: (b, COL_SB_K // wide + h)),
            pl.BlockSpec((seq, wide), lambda b, h, i: (b, COL_SB_V // wide + h)),
            pl.BlockSpec((1, wide), lambda b, h, i: (0, h)),
        ],
        out_specs=pl.BlockSpec((SB_BLOCK, wide), lambda b, h, i: (b * nq + i, h)),
        out_shape=jax.ShapeDtypeStruct((batch * seq, W_SB), BF16),
        compiler_params=_params(("parallel", "parallel", "arbitrary"), VMEM_LIMIT),
        name="sb_prompt",
    )(proj, proj, proj, norm_g)


def _sb_sample_kernel(q_ref, k_ref, v_ref, kp_ref, vp_ref, g_ref, o_ref, *, seq, n_past_blocks):
    suffix_and_total = _sb_suffix_and_total()
    causal = _sb_causal(seq)
    pad = jnp.zeros((SB_BLOCK - seq, SB_HEAD), BF16)
    zero = jnp.zeros((seq, SB_BLOCK), F32)

    def past_block(rows, j):
        return kp_ref[rows, j, :].astype(BF16), vp_ref[rows, j, :].astype(BF16)

    def new_block(j):
        return (jnp.concatenate([k_ref[:, _head_lanes(j)].astype(BF16), pad], axis=0),
                jnp.concatenate([v_ref[:, _head_lanes(j)].astype(BF16), pad], axis=0))

    heads = range(SB_HEADS)
    qs = [q_ref[:, _head_lanes(j)].astype(BF16) for j in heads]
    runs, accs = _sb_blocks(qs, [new_block(j) for j in heads], [zero] * len(heads), [zero] * len(heads),
                            suffix_and_total, causal)
    _sb_finish(_sb_sweep(qs, past_block, n_past_blocks, runs, accs, suffix_and_total), g_ref, o_ref)


def _sb_sample(proj, row0, batch, seq, layer, cache_k, cache_v, norm_g):
    past = cache_k.shape[2]
    rb0 = row0 // seq
    kern = functools.partial(_sb_sample_kernel, seq=seq, n_past_blocks=past // SB_BLOCK)
    cache_spec = pl.BlockSpec((None, None, past, SB_HEADS, SB_HEAD), lambda b: (layer, b, 0, 0, 0))
    return pl.pallas_call(
        kern,
        grid=(batch,),
        in_specs=[
            pl.BlockSpec((seq, W_SB), lambda b: (rb0 + b, COL_SB_Q // W_SB)),
            pl.BlockSpec((seq, W_SB), lambda b: (rb0 + b, COL_SB_K // W_SB)),
            pl.BlockSpec((seq, W_SB), lambda b: (rb0 + b, COL_SB_V // W_SB)),
            cache_spec, cache_spec,
            pl.BlockSpec((1, W_SB), lambda b: (0, 0)),
        ],
        out_specs=pl.BlockSpec((seq, W_SB), lambda b: (b, 0)),
        out_shape=jax.ShapeDtypeStruct((batch * seq, W_SB), BF16),
        compiler_params=_params(("parallel",), VMEM_LIMIT),
        name="sb_sample",
    )(proj, proj, proj, cache_k, cache_v, norm_g)


def _rwkv_prep_kernel(p_ref, bnd_ref, mu_ref, w0_ref, a0_ref, w2_ref, a2_ref, g2_ref,
                      r_out, w_out, k_out, v_out, a_out, g_out):
    p = p_ref[...]
    first = lax.broadcasted_iota(jnp.int32, p.shape, 0) == 0
    prev = jnp.where(first, bnd_ref[...], pltpu.roll(p, 1, 0))
    pm = p + (prev - p) * mu_ref[...]
    o = 0
    r = pm[:, o:o + W_RWKV]; o += W_RWKV
    w_low = pm[:, o:o + D_LORA]; o += D_LORA
    k = pm[:, o:o + W_RWKV]; o += W_RWKV
    v = pm[:, o:o + W_RWKV]; o += W_RWKV
    a_low = pm[:, o:o + D_LORA]; o += D_LORA
    g_low = pm[:, o:o + D_GATE_PAD]
    log_w = -math.exp(-0.5) * jax.nn.sigmoid(
        w0_ref[...] + jnp.dot(jnp.tanh(w_low).astype(BF16), w2_ref[...], preferred_element_type=F32))
    a = jax.nn.sigmoid(a0_ref[...] + jnp.dot(a_low.astype(BF16), a2_ref[...], preferred_element_type=F32))
    g = jnp.dot(jax.nn.sigmoid(g_low).astype(BF16), g2_ref[...], preferred_element_type=F32)
    r_out[...] = r
    w_out[...] = jnp.exp(log_w)
    k_out[...] = k
    v_out[...] = v
    a_out[...] = a
    g_out[...] = g


def _rwkv_prep(proj, row0, batch, seq, shift, mu, w0, a0, w2, a2, g2):
    tm = _tile(seq, (128, 64, 32, 16, 8))
    nt = seq // tm
    m = batch * seq
    rb0 = row0 // tm
    last = proj[row0 + tm - 1:row0 + m:tm, :P_RWKV_PAD].reshape(batch, nt, P_RWKV_PAD)
    bnd = jnp.concatenate([shift[:, None, :], last[:, :nt - 1]], axis=1).reshape(batch * nt, 1, P_RWKV_PAD)
    row = lambda i: (i, 0)
    fixed = lambda i: (0, 0)
    vec = pl.BlockSpec((1, W_RWKV), fixed)
    out = pl.BlockSpec((tm, W_RWKV), row)
    return pl.pallas_call(
        _rwkv_prep_kernel,
        grid=(m // tm,),
        in_specs=[pl.BlockSpec((tm, P_RWKV_PAD), lambda i: (rb0 + i, 0)),
                  pl.BlockSpec((None, 1, P_RWKV_PAD), lambda i: (i, 0, 0)),
                  pl.BlockSpec((1, P_RWKV_PAD), fixed), vec, vec,
                  pl.BlockSpec((D_LORA, W_RWKV), fixed), pl.BlockSpec((D_LORA, W_RWKV), fixed),
                  pl.BlockSpec((D_GATE_PAD, W_RWKV), fixed)],
        out_specs=[out] * 6,
        out_shape=[jax.ShapeDtypeStruct((m, W_RWKV), F32)] * 6,
        compiler_params=_params(("parallel",), VMEM_LIMIT),
        name="rwkv_prep",
    )(proj, bnd, mu, w0, a0, w2, a2, g2)


def _spread_lanes(copies):
    lanes = copies[0].shape[-1]
    full = lanes // LANE
    pieces = [c[..., i * LANE:(i + 1) * LANE] for i in range(full) for c in copies]
    if lanes % LANE:
        pieces += [c[..., full * LANE:] for c in copies]
    return pieces[0] if len(pieces) == 1 else jnp.concatenate(pieces, axis=-1)


def _gather_lanes(x, n_copies):
    lanes = x.shape[-1] // n_copies
    full, rem = lanes // LANE, lanes % LANE
    out = []
    for g in range(n_copies):
        pieces = [x[..., (i * n_copies + g) * LANE:(i * n_copies + g + 1) * LANE] for i in range(full)]
        if rem:
            start = full * n_copies * LANE + g * rem
            pieces.append(x[..., start:start + rem])
        out.append(pieces[0] if len(pieces) == 1 else jnp.concatenate(pieces, axis=-1))
    return out


def _rwkv_scan_kernel(r_ref, w_ref, k_ref, v_ref, a_ref, rk_ref, lng_ref, lnb_ref, kkp_ref, kap_ref, s0_ref,
                      z_ref, sT_ref, s_scr, kk_scr, b_scr, w_scr, k_scr, r_scr, v_scr, y_scr,
                      *, t_block, n_steps, vs):
    c = pl.program_id(1)
    n = RWKV_HEAD
    nv = n // vs
    lanes_in = r_ref.shape[-1]
    lanes_work = lanes_in * vs

    def dup(x):
        return _spread_lanes([x] * vs)

    @pl.when(c == 0)
    def _():
        s_scr[...] = s0_ref[...]

    def step(t, carry):
        k_raw = k_ref[t]
        a = a_ref[t]
        kk_raw = k_raw * kkp_ref[...]
        norm = jnp.sqrt(jnp.sum(kk_raw * kk_raw, axis=0, keepdims=True))
        kk = kk_raw / jnp.maximum(norm, 1e-12)
        k_t = k_raw * (1.0 + (a - 1.0) * kap_ref[...])
        kk_scr[...] = dup(kk)
        b_scr[...] = dup(kk * a)
        w_scr[...] = dup(w_ref[t])
        k_scr[...] = dup(k_t)
        r_scr[...] = dup(r_ref[t])
        v_t = v_ref[t]
        v_scr[...] = _spread_lanes([v_t[g * nv:(g + 1) * nv] for g in range(vs)])

        for lg in range(lanes_work // LANE):
            ls = pl.ds(lg * LANE, LANE)

            def value_rows(vo, carry2):
                v0 = pl.multiple_of(vo * SUBLANE, SUBLANE)
                v_tile = v_scr[pl.ds(v0, SUBLANE), ls]
                y_rows = []
                for vi in range(SUBLANE):
                    s_v = s_scr[v0 + vi, :, ls]
                    sa = jnp.sum(s_v * kk_scr[:, ls], axis=0, keepdims=True)
                    s_new = s_v * w_scr[:, ls] - sa * b_scr[:, ls] + v_tile[vi:vi + 1, :] * k_scr[:, ls]
                    s_scr[v0 + vi, :, ls] = s_new
                    y_rows.append(jnp.sum(s_new * r_scr[:, ls], axis=0, keepdims=True))
                y_scr[pl.ds(v0, SUBLANE), ls] = jnp.concatenate(y_rows, axis=0)
                return carry2

            lax.fori_loop(0, nv // SUBLANE, value_rows, 0)

        y_groups = _gather_lanes(y_scr[...], vs)
        y = y_groups[0] if vs == 1 else jnp.concatenate(y_groups, axis=0)
        mean = jnp.mean(y, axis=0, keepdims=True)
        d = y - mean
        var = jnp.mean(d * d, axis=0, keepdims=True)
        bonus = jnp.sum(r_ref[t] * k_t * rk_ref[...], axis=0, keepdims=True) * v_t
        z_ref[t] = d * lax.rsqrt(var + RWKV_GN_EPS) * lng_ref[...] + lnb_ref[...] + bonus
        return carry

    lax.fori_loop(0, t_block, step, 0)

    @pl.when(c == n_steps - 1)
    def _():
        sT_ref[...] = s_scr[...]


def _rwkv_scan(r, w, k, v, a, rk, lng, lnb, kkp, kap, s0, vs):
    seq, n, lanes = r.shape
    nv = n // vs
    t_block = _tile(seq, (8, 4, 2, 1))
    n_steps = seq // t_block
    lb = lanes if vs > 1 else _tile(lanes, (3 * LANE, 2 * LANE, LANE))
    lbw = lb * vs
    seq_spec = pl.BlockSpec((t_block, n, lb), lambda g, c: (c, 0, g))
    vec_spec = pl.BlockSpec((n, lb), lambda g, c: (0, g))
    st_spec = pl.BlockSpec((nv, n, lbw), lambda g, c: (0, 0, g))
    kern = functools.partial(_rwkv_scan_kernel, t_block=t_block, n_steps=n_steps, vs=vs)
    work = lambda rows: pltpu.VMEM((rows, lbw), F32)
    return pl.pallas_call(
        kern,
        grid=(lanes // lb, n_steps),
        in_specs=[seq_spec] * 5 + [vec_spec] * 5 + [st_spec],
        out_specs=[seq_spec, st_spec],
        out_shape=[jax.ShapeDtypeStruct((seq, n, lanes), F32), jax.ShapeDtypeStruct((nv, n, lanes * vs), F32)],
        scratch_shapes=[pltpu.VMEM((nv, n, lbw), F32), work(n), work(n), work(n), work(n), work(n),
                        work(nv), work(nv)],
        compiler_params=_params(("parallel", "arbitrary"), VMEM_LIMIT),
        name="rwkv_scan",
    )(r, w, k, v, a, rk, lng, lnb, kkp, kap, s0)


def _merge_kernel(gla_p, z_p, g_p, sb_p, gla_s, z_s, g_s, sb_s, o_ref, *, n_prompt_tiles):
    def write(gla_ref, z_ref, g_ref, sb_ref):
        o_ref[:, 0:W_GLA] = gla_ref[...]
        o_ref[:, W_GLA:W_GLA + W_RWKV] = (z_ref[...] * g_ref[...]).astype(o_ref.dtype)
        o_ref[:, W_GLA + W_RWKV:D_MODEL] = sb_ref[...]

    i = pl.program_id(0)

    @pl.when(i < n_prompt_tiles)
    def _():
        write(gla_p, z_p, g_p, sb_p)

    @pl.when(i >= n_prompt_tiles)
    def _():
        write(gla_s, z_s, g_s, sb_s)


def _merge(prompt, sample):
    mp, ms = prompt[1].shape[0], sample[1].shape[0]
    tm = _tile(math.gcd(mp, ms), (512, 256, 128, 64, 32, 16))
    n_p, n_s = mp // tm, ms // tm
    widths = (W_GLA, W_RWKV, W_RWKV, W_SB)
    p_specs = [pl.BlockSpec((tm, wd), lambda i: (jnp.minimum(i, n_p - 1), 0)) for wd in widths]
    s_specs = [pl.BlockSpec((tm, wd), lambda i: (jnp.maximum(i - n_p, 0), 0)) for wd in widths]
    return pl.pallas_call(
        functools.partial(_merge_kernel, n_prompt_tiles=n_p),
        grid=(n_p + n_s,),
        in_specs=p_specs + s_specs,
        out_specs=pl.BlockSpec((tm, D_MODEL), lambda i: (i, 0)),
        out_shape=jax.ShapeDtypeStruct((mp + ms, D_MODEL), BF16),
        compiler_params=_params(("parallel",), VMEM_LIMIT),
        name="merge",
    )(*prompt, *sample)


def _to_scan_layout(x, batch, seq):
    x = x.reshape(batch, seq, RWKV_HEADS, RWKV_HEAD).transpose(1, 3, 0, 2)
    return x.reshape(seq, RWKV_HEAD, batch * RWKV_HEADS)


def _from_scan_layout(z, batch, seq):
    z = z.reshape(seq, RWKV_HEAD, batch, RWKV_HEADS)
    return z.transpose(2, 0, 3, 1).reshape(batch * seq, W_RWKV)


def _head_vec_to_scan_layout(x, batch):
    x = x.reshape(RWKV_HEADS, RWKV_HEAD).T
    return jnp.tile(x[:, None, :], (1, batch, 1)).reshape(RWKV_HEAD, batch * RWKV_HEADS)


def _rwkv_group(ins, batch, seq, params, s0):
    lanes = batch * RWKV_HEADS
    n = RWKV_HEAD
    vs = next(f for f in (1, 2, 4, 8) if (lanes * f) % LANE == 0)
    nv = n // vs
    s0_l = s0.transpose(2, 3, 0, 1).reshape(n, n, lanes)
    s0_l = _spread_lanes([s0_l[g * nv:(g + 1) * nv] for g in range(vs)])
    vecs = [_head_vec_to_scan_layout(x, batch) for x in params]
    z, s_t = _rwkv_scan(*ins, *vecs, s0_l, vs)
    s_new = jnp.concatenate(_gather_lanes(s_t, vs), axis=0).reshape(n, n, batch, RWKV_HEADS).transpose(2, 3, 0, 1)
    return _from_scan_layout(z, batch, seq), s_new


W_IN_SEGMENTS = (
    (COL_RWKV, P_GLA, P_RWKV, P_RWKV_PAD - P_RWKV),
    (COL_GLA_Q, 0, GLA_QK, 0),
    (COL_SB_Q, P_GLA + P_RWKV, P_SB, 0),
    (COL_GLA_V, 2 * GLA_QK, W_GLA, 0),
    (COL_GLA_G, 2 * GLA_QK + W_GLA, W_GLA, 0),
    (COL_GLA_K, GLA_QK, GLA_QK, 0),
    (COL_GLA_A, 2 * GLA_QK + 2 * W_GLA, GLA_LOWRANK, P_PAD - COL_GLA_A - GLA_LOWRANK),
)


def _rearrange_w_in_kernel(w_ref, o_ref):
    rows = w_ref.shape[0]
    for dst, src, width, pad in W_IN_SEGMENTS:
        piece = w_ref[:, src:src + width].astype(BF16)
        if pad:
            piece = jnp.concatenate([piece, jnp.zeros((rows, pad), BF16)], axis=1)
        o_ref[:, dst:dst + width + pad] = piece


def _rearrange_w_in(w_in, layer):
    _, d, p = w_in.shape
    tk = _tile(d, (64, 32, 16))
    return pl.pallas_call(
        _rearrange_w_in_kernel,
        grid=(d // tk,),
        in_specs=[pl.BlockSpec((None, tk, p), lambda i: (layer, i, 0))],
        out_specs=pl.BlockSpec((tk, P_PAD), lambda i: (i, 0)),
        out_shape=jax.ShapeDtypeStruct((d, P_PAD), BF16),
        compiler_params=_params(("parallel",), VMEM_LIMIT),
        name="rearrange_w_in",
    )(w_in)


def _cast_pad_cols_kernel(w_ref, o_ref):
    rows, n = w_ref.shape
    pad = o_ref.shape[1] - n
    o_ref[:, :n] = w_ref[...].astype(BF16)
    if pad:
        o_ref[:, n:] = jnp.zeros((rows, pad), BF16)


def _cast_pad_cols(w, layer, n_out):
    _, k, n = w.shape
    tk = _tile(k, (128, 64, 32, 16))
    return pl.pallas_call(
        _cast_pad_cols_kernel,
        grid=(k // tk,),
        in_specs=[pl.BlockSpec((None, tk, n), lambda i: (layer, i, 0))],
        out_specs=pl.BlockSpec((tk, n_out), lambda i: (i, 0)),
        out_shape=jax.ShapeDtypeStruct((k, n_out), BF16),
        compiler_params=_params(("parallel",), VMEM_LIMIT),
        name="cast_pad_cols",
    )(w)


def _cast_pad_rows_kernel(w_ref, o_ref, *, n_data_tiles):
    @pl.when(pl.program_id(0) < n_data_tiles)
    def _():
        o_ref[...] = w_ref[...].astype(BF16)

    @pl.when(pl.program_id(0) >= n_data_tiles)
    def _():
        o_ref[...] = jnp.zeros(o_ref.shape, BF16)


def _cast_pad_rows(w, layer, k_out):
    _, k, n = w.shape
    tk = math.gcd(k, k_out - k)
    n_data = k // tk
    return pl.pallas_call(
        functools.partial(_cast_pad_rows_kernel, n_data_tiles=n_data),
        grid=(k_out // tk,),
        in_specs=[pl.BlockSpec((None, tk, n), lambda i: (layer, jnp.minimum(i, n_data - 1), 0))],
        out_specs=pl.BlockSpec((tk, n), lambda i: (i, 0)),
        out_shape=jax.ShapeDtypeStruct((k_out, n), BF16),
        compiler_params=_params(("parallel",), VMEM_LIMIT),
        name="cast_pad_rows",
    )(w)


def _heads_out_kernel(x_ref, o_ref):
    for h in range(SB_HEADS):
        o_ref[:, h, :] = x_ref[:, _head_lanes(h)]


def _heads_out(proj, col0, row0, rows):
    tm = _tile(rows, (512, 256, 128, 64, 32, 16, 8))
    rb0 = row0 // tm
    return pl.pallas_call(
        _heads_out_kernel,
        grid=(rows // tm,),
        in_specs=[pl.BlockSpec((tm, W_SB), lambda i: (rb0 + i, col0 // W_SB))],
        out_specs=pl.BlockSpec((tm, SB_HEADS, SB_HEAD), lambda i: (i, 0, 0)),
        out_shape=jax.ShapeDtypeStruct((rows, SB_HEADS, SB_HEAD), F32),
        compiler_params=_params(("parallel",), VMEM_LIMIT),
        name="heads_out",
    )(proj)


def kernel(x_prompt, x_sample, state_gla, state_rwkv, state_rwkv_shift, cache_sb_k, cache_sb_v, norm1_g, w_in, gla_wa2, gla_ba, gla_norm_g, rwkv_mu, rwkv_w0, rwkv_w2, rwkv_a0, rwkv_a2, rwkv_g2, rwkv_k_k, rwkv_k_a, rwkv_r_k, rwkv_ln_g, rwkv_ln_b, sb_norm_g, w_out, norm2_g, w_gate, w_up, w_down, final_g):
    bp, tp, d = x_prompt.shape
    bs, ts, _ = x_sample.shape
    depth = w_in.shape[0]
    mp, ms = bp * tp, bs * ts
    x = jnp.concatenate([x_prompt.reshape(mp, d), x_sample.reshape(ms, d)], axis=0)

    outs = {name: [] for name in ("gla_p", "gla_s", "rw_p", "rw_s", "sh_p", "sh_s", "k_p", "k_s", "v_p", "v_s")}
    row1 = lambda a: a.reshape(1, -1)
    for l in range(depth):
        h = _rmsnorm(x, norm1_g[l], BF16)
        proj = _matmul(h, _rearrange_w_in(w_in, l), F32)
        outs["sh_p"].append(proj[tp - 1:mp:tp, :P_RWKV].reshape(bp, 1, P_RWKV))
        outs["sh_s"].append(proj[mp + ts - 1::ts, :P_RWKV].reshape(bs, 1, P_RWKV))
        for nm, c0 in (("k", COL_SB_K), ("v", COL_SB_V)):
            outs[nm + "_p"].append(_heads_out(proj, c0, 0, mp).reshape(bp, tp, SB_HEADS, SB_HEAD))
            outs[nm + "_s"].append(_heads_out(proj, c0, mp, ms).reshape(bs, ts, SB_HEADS, SB_HEAD))

        lane_pad = (0, P_RWKV_PAD - P_RWKV)
        prep_args = (row1(jnp.pad(rwkv_mu[l], lane_pad)), row1(rwkv_w0[l]), row1(rwkv_a0[l]),
                     rwkv_w2[l].astype(BF16), rwkv_a2[l].astype(BF16),
                     jnp.pad(rwkv_g2[l], ((0, D_GATE_PAD - D_GATE_LORA), (0, 0))).astype(BF16))
        vec_args = (rwkv_r_k[l].reshape(-1), rwkv_ln_g[l], rwkv_ln_b[l], rwkv_k_k[l], rwkv_k_a[l])
        *seqs_p, gate_p = _rwkv_prep(proj, 0, bp, tp, jnp.zeros((bp, P_RWKV_PAD), F32), *prep_args)
        *seqs_s, gate_s = _rwkv_prep(proj, mp, bs, ts, jnp.pad(state_rwkv_shift[l][:, 0, :], ((0, 0), lane_pad)),
                                     *prep_args)
        scan_in_p = [_to_scan_layout(s, bp, tp) for s in seqs_p]
        scan_in_s = [_to_scan_layout(s, bs, ts) for s in seqs_s]

        wa_pad = jnp.pad(gla_wa2[l], ((0, LANE - GLA_LOWRANK), (0, 0))).astype(BF16)
        gla_args = (wa_pad, row1(gla_ba[l]), row1(gla_norm_g[l]))
        o_gla_p, sg_p = _gla(proj, 0, bp, tp, *gla_args, jnp.zeros((bp, GLA_HEADS, GLA_DV, GLA_DK), F32))
        o_gla_s, sg_s = _gla(proj, mp, bs, ts, *gla_args, state_gla[l].swapaxes(-1, -2))
        outs["gla_p"].append(sg_p.swapaxes(-1, -2))
        outs["gla_s"].append(sg_s.swapaxes(-1, -2))

        sbg = row1(sb_norm_g[l])
        o_sb_p = _sb_prompt(proj, bp, tp, sbg)
        o_sb_s = _sb_sample(proj, mp, bs, ts, l, cache_sb_k, cache_sb_v, sbg)

        z_p, sr_p = _rwkv_group(scan_in_p, bp, tp, vec_args, jnp.zeros((bp, RWKV_HEADS, RWKV_HEAD, RWKV_HEAD), F32))
        z_s, sr_s = _rwkv_group(scan_in_s, bs, ts, vec_args, state_rwkv[l])
        outs["rw_p"].append(sr_p)
        outs["rw_s"].append(sr_s)

        merged = _merge((o_gla_p, z_p, gate_p, o_sb_p), (o_gla_s, z_s, gate_s, o_sb_s))
        x = _matmul_residual(merged, w_out[l].astype(BF16), x, tk=d)

        h2 = _rmsnorm(x, norm2_g[l], BF16)
        act = _swiglu(h2, _cast_pad_cols(w_gate, l, D_FF_PAD), _cast_pad_cols(w_up, l, D_FF_PAD))
        x = _matmul_residual(act, _cast_pad_rows(w_down, l, D_FF_PAD), x, tk=D_FF_PAD // 4)

    y_p, y_s = _rmsnorm_split(x, final_g, mp)
    sd = state_gla.dtype
    st = lambda name, dt=None: jnp.stack(outs[name]) if dt is None else jnp.stack(outs[name]).astype(dt)
    return (y_p.reshape(bp, tp, d), y_s.reshape(bs, ts, d),
            st("gla_p", sd), st("gla_s", sd), st("rw_p", sd), st("rw_s", sd),
            st("sh_p"), st("sh_s"), st("k_p"), st("k_s"), st("v_p"), st("v_s"))
```

```python
import functools
import math

import jax
import jax.numpy as jnp
import numpy as np
from jax import lax
from jax.experimental import pallas as pl
from jax.experimental.pallas import tpu as pltpu

F32 = jnp.float32
BF16 = jnp.bfloat16

D_MODEL = 4096
NORM_EPS = 1e-5
W_GLA = 1536
W_RWKV = 1536
W_SB = 1024
GLA_HEADS = 6
GLA_DV = 256
GLA_DK = 128
GLA_QK = 768
GLA_LOWRANK = 16
GLA_TAU = 16.0
GLA_CHUNK = 64
GLA_HEADS_PER_STEP = 3
RWKV_HEAD = 64
RWKV_HEADS = 24
D_LORA = 128
D_GATE_LORA = 480
D_GATE_PAD = 512
RWKV_GN_EPS = 64e-5
SB_HEAD = 128
SB_HEADS = 8
SB_SCALE = SB_HEAD ** -0.5
SB_BLOCK = 128
SB_HEADS_PER_STEP = 8
D_FF = 11008
D_FF_PAD = 11264
P_GLA = 4624
P_RWKV = 5344
P_RWKV_PAD = 5376
P_SB = 3072

COL_RWKV = 0
COL_GLA_Q = P_RWKV_PAD
COL_SB_Q = COL_GLA_Q + GLA_QK
COL_SB_K = COL_SB_Q + W_SB
COL_SB_V = COL_SB_K + W_SB
COL_GLA_V = COL_SB_V + W_SB
COL_GLA_G = COL_GLA_V + W_GLA
COL_GLA_K = COL_GLA_G + W_GLA
COL_GLA_A = COL_GLA_K + GLA_QK
P_PAD = 13312

LANE = 128
SUBLANE = 8
VMEM_LIMIT = 56 * 1024 * 1024

EXP_ZERO_BELOW = -104.0


def _tile(n, candidates):
    for c in candidates:
        if n % c == 0:
            return c
    return n


def _params(sem, vmem=None):
    return pltpu.CompilerParams(dimension_semantics=sem, vmem_limit_bytes=vmem)


def _rmsnorm_kernel(x_ref, g_ref, o_ref):
    x = x_ref[...]
    ms = jnp.mean(x * x, axis=-1, keepdims=True)
    o_ref[...] = (x * lax.rsqrt(ms + NORM_EPS) * g_ref[...]).astype(o_ref.dtype)


def _rmsnorm(x, g, out_dtype):
    m, d = x.shape
    tm = _tile(m, (512, 256, 128, 64, 32, 16, 8))
    return pl.pallas_call(
        _rmsnorm_kernel,
        grid=(m // tm,),
        in_specs=[pl.BlockSpec((tm, d), lambda i: (i, 0)), pl.BlockSpec((1, d), lambda i: (0, 0))],
        out_specs=pl.BlockSpec((tm, d), lambda i: (i, 0)),
        out_shape=jax.ShapeDtypeStruct((m, d), out_dtype),
        compiler_params=_params(("parallel",), VMEM_LIMIT),
        name="rmsnorm",
    )(x, g.reshape(1, d))


def _two_part_specs(tm, width, n_first, col=None):
    col = col or (lambda *idx: 0)
    return [pl.BlockSpec((tm, width), lambda *idx: (jnp.minimum(idx[0], n_first - 1), col(*idx))),
            pl.BlockSpec((tm, width), lambda *idx: (jnp.maximum(idx[0] - n_first, 0), col(*idx)))]


def _rmsnorm_joined_kernel(a_ref, b_ref, g_ref, o_ref, *, n_first):
    def run(x_ref):
        _rmsnorm_kernel(x_ref, g_ref, o_ref)

    @pl.when(pl.program_id(0) < n_first)
    def _():
        run(a_ref)

    @pl.when(pl.program_id(0) >= n_first)
    def _():
        run(b_ref)


def _rmsnorm_joined(a, b, g, out_dtype):
    (ma, d), mb = a.shape, b.shape[0]
    tm = _tile(math.gcd(ma, mb), (512, 256, 128, 64, 32, 16, 8))
    n_first = ma // tm
    return pl.pallas_call(
        functools.partial(_rmsnorm_joined_kernel, n_first=n_first),
        grid=((ma + mb) // tm,),
        in_specs=_two_part_specs(tm, d, n_first) + [pl.BlockSpec((1, d), lambda i: (0, 0))],
        out_specs=pl.BlockSpec((tm, d), lambda i: (i, 0)),
        out_shape=jax.ShapeDtypeStruct((ma + mb, d), out_dtype),
        compiler_params=_params(("parallel",), VMEM_LIMIT),
        name="rmsnorm_joined",
    )(a, b, g.reshape(1, d))


def _rmsnorm_split_kernel(x_ref, g_ref, a_ref, b_ref, *, n_first):
    x = x_ref[...]
    ms = jnp.mean(x * x, axis=-1, keepdims=True)
    y = x * lax.rsqrt(ms + NORM_EPS) * g_ref[...]
    i = pl.program_id(0)

    @pl.when(i < n_first)
    def _():
        a_ref[...] = y

    @pl.when(i >= n_first)
    def _():
        b_ref[...] = y


def _rmsnorm_split(x, g, m_first):
    m, d = x.shape
    tm = _tile(math.gcd(m_first, m - m_first), (512, 256, 128, 64, 32, 16, 8))
    n_first = m_first // tm
    return pl.pallas_call(
        functools.partial(_rmsnorm_split_kernel, n_first=n_first),
        grid=(m // tm,),
        in_specs=[pl.BlockSpec((tm, d), lambda i: (i, 0)), pl.BlockSpec((1, d), lambda i: (0, 0))],
        out_specs=[pl.BlockSpec((tm, d), lambda i: (jnp.minimum(i, n_first - 1), 0)),
                   pl.BlockSpec((tm, d), lambda i: (jnp.maximum(i - n_first, 0), 0))],
        out_shape=[jax.ShapeDtypeStruct((m_first, d), F32), jax.ShapeDtypeStruct((m - m_first, d), F32)],
        compiler_params=_params(("arbitrary",), VMEM_LIMIT),
        name="rmsnorm_split",
    )(x, g.reshape(1, d))


def _mm_kernel(x_ref, w_ref, o_ref):
    o_ref[...] = jnp.dot(x_ref[...], w_ref[...], preferred_element_type=F32).astype(o_ref.dtype)


def _mm_res_kernel(x_ref, w_ref, r_ref, o_ref):
    acc = jnp.dot(x_ref[...], w_ref[...], preferred_element_type=F32)
    k = pl.program_id(2)

    @pl.when(k == 0)
    def _():
        o_ref[...] = r_ref[...] + acc

    @pl.when(k != 0)
    def _():
        o_ref[...] += acc


def _swiglu_kernel(x_ref, wg_ref, wu_ref, o_ref):
    x = x_ref[...]
    g = jnp.dot(x, wg_ref[...], preferred_element_type=F32)
    u = jnp.dot(x, wu_ref[...], preferred_element_type=F32)
    o_ref[...] = (g * jax.nn.sigmoid(g) * u).astype(o_ref.dtype)


def _matmul(x, w, out_dtype):
    m, k = x.shape
    n = w.shape[1]
    tm = _tile(m, (1024, 512, 256, 128, 64, 32, 16, 8))
    tn = _tile(n, (1024, 512, 256, 128))
    return pl.pallas_call(
        _mm_kernel,
        grid=(m // tm, n // tn),
        in_specs=[pl.BlockSpec((tm, k), lambda i, j: (i, 0)), pl.BlockSpec((k, tn), lambda i, j: (0, j))],
        out_specs=pl.BlockSpec((tm, tn), lambda i, j: (i, j)),
        out_shape=jax.ShapeDtypeStruct((m, n), out_dtype),
        compiler_params=_params(("parallel", "parallel"), VMEM_LIMIT),
        name="matmul",
    )(x, w)


def _matmul_residual(x, w, res, tk):
    m, k = x.shape
    n = w.shape[1]
    tm = _tile(m, (1024, 512, 256, 128, 64, 32, 16, 8))
    tn = _tile(n, (1024, 512, 256, 128))
    return pl.pallas_call(
        _mm_res_kernel,
        grid=(m // tm, n // tn, k // tk),
        in_specs=[pl.BlockSpec((tm, tk), lambda i, j, kk: (i, kk)),
                  pl.BlockSpec((tk, tn), lambda i, j, kk: (kk, j)),
                  pl.BlockSpec((tm, tn), lambda i, j, kk: (i, j))],
        out_specs=pl.BlockSpec((tm, tn), lambda i, j, kk: (i, j)),
        out_shape=jax.ShapeDtypeStruct((m, n), F32),
        compiler_params=_params(("parallel", "parallel", "arbitrary"), VMEM_LIMIT),
        name="matmul_residual",
    )(x, w, res)


def _mm_res_joined_kernel(x_ref, w_ref, ra_ref, rb_ref, o_ref, *, n_first):
    acc = jnp.dot(x_ref[...], w_ref[...], preferred_element_type=F32)

    @pl.when(pl.program_id(0) < n_first)
    def _():
        o_ref[...] = ra_ref[...] + acc

    @pl.when(pl.program_id(0) >= n_first)
    def _():
        o_ref[...] = rb_ref[...] + acc


def _matmul_residual_joined(x, w, res_a, res_b):
    m, k = x.shape
    n = w.shape[1]
    ma = res_a.shape[0]
    tm = _tile(math.gcd(ma, m - ma), (1024, 512, 256, 128, 64, 32, 16, 8))
    tn = _tile(n, (512, 256, 128))
    n_first = ma // tm
    return pl.pallas_call(
        functools.partial(_mm_res_joined_kernel, n_first=n_first),
        grid=(m // tm, n // tn),
        in_specs=[pl.BlockSpec((tm, k), lambda i, j: (i, 0)), pl.BlockSpec((k, tn), lambda i, j: (0, j))]
        + _two_part_specs(tm, tn, n_first, col=lambda i, j: j),
        out_specs=pl.BlockSpec((tm, tn), lambda i, j: (i, j)),
        out_shape=jax.ShapeDtypeStruct((m, n), F32),
        compiler_params=_params(("parallel", "parallel"), VMEM_LIMIT),
        name="matmul_residual_joined",
    )(x, w, res_a, res_b)


def _swiglu(x, wg, wu):
    m, k = x.shape
    n = wg.shape[1]
    tm = _tile(m, (1024, 512, 256, 128, 64, 32, 16, 8))
    tn = _tile(n, (512, 256, 128))
    return pl.pallas_call(
        _swiglu_kernel,
        grid=(m // tm, n // tn),
        in_specs=[pl.BlockSpec((tm, k), lambda i, j: (i, 0)),
                  pl.BlockSpec((k, tn), lambda i, j: (0, j)),
                  pl.BlockSpec((k, tn), lambda i, j: (0, j))],
        out_specs=pl.BlockSpec((tm, tn), lambda i, j: (i, j)),
        out_shape=jax.ShapeDtypeStruct((m, n), BF16),
        compiler_params=_params(("parallel", "parallel"), VMEM_LIMIT),
        name="swiglu",
    )(x, wg, wu)


def _log_sigmoid(x):
    return jnp.minimum(x, 0.0) - jnp.log1p(jnp.exp(-jnp.abs(x)))


def _split_bf16(x):
    hi = x.astype(BF16)
    lo = (x - hi.astype(F32)).astype(BF16)
    return hi, lo


def _dot_nt(a, b):
    return lax.dot_general(a, b, (((1,), (1,)), ((), ())), preferred_element_type=F32)


def _dot_tn(a, b):
    return lax.dot_general(a, b, (((0,), (0,)), ((), ())), preferred_element_type=F32)


def _gla_tables(chunk):
    levels = chunk.bit_length() - 1
    idx = np.arange(chunk)
    prefix = np.zeros((1 + 2 * levels, chunk, chunk), np.float32)
    prefix[0] = idx[None, :] <= idx[:, None]
    for lv in range(levels):
        m = 1 << lv
        mid = (idx // (2 * m)) * 2 * m + m
        for t in range(chunk):
            if t >= mid[t]:
                prefix[1 + lv, t, mid[t]:t + 1] = 1.0
            else:
                prefix[1 + levels + lv, t, t + 1:mid[t]] = 1.0
    x = idx[:, None] ^ idx[None, :]
    level = np.where(idx[:, None] == idx[None, :], levels, np.floor(np.log2(np.maximum(x, 1))).astype(np.int32))
    level = np.where(idx[None, :] > idx[:, None], -1, level).astype(np.int32)
    return prefix.reshape(-1, chunk), level, levels


def _gla_kernel(q_ref, k_ref, v_ref, g_ref, al_ref, wa_ref, ba_ref, ng_ref, pre_ref, lvl_ref, s0_ref,
                o_ref, sT_ref, s_scr, *, chunk, levels, n_chunks, n_steps):
    c = pl.program_id(2)

    @pl.when(c == 0)
    def _():
        s_scr[...] = s0_ref[...]

    prefix = pre_ref[...]
    level = lvl_ref[...]
    ng = ng_ref[...]
    x = jnp.dot(al_ref[...].astype(BF16), wa_ref[...], preferred_element_type=F32) + ba_ref[...]
    hi, lo = _split_bf16(_log_sigmoid(x) * (1.0 / GLA_TAU))
    width = GLA_HEADS_PER_STEP * GLA_DK

    for ci in range(n_chunks):
        rows = pl.ds(ci * chunk, chunk)
        r0 = ci * chunk
        parts = jnp.dot(prefix, jnp.concatenate([hi[r0:r0 + chunk], lo[r0:r0 + chunk]], axis=1),
                        preferred_element_type=F32)
        sums_all = parts[:, :width] + parts[:, width:]
        for j in range(GLA_HEADS_PER_STEP):
            kl = slice(j * GLA_DK, (j + 1) * GLA_DK)
            vl = slice(j * GLA_DV, (j + 1) * GLA_DV)
            sums = sums_all[:, kl]
            b = sums[:chunk]
            b_last = b[chunk - 1:chunk, :]
            decay = jnp.exp(sums[chunk:])
            q = q_ref[rows, kl] * (GLA_DK ** -0.5)
            k = k_ref[rows, kl]
            v = v_ref[rows, vl].astype(BF16)
            scores = jnp.where(level == levels, _dot_nt(q.astype(BF16), k.astype(BF16)), 0.0)
            for lv in range(levels):
                q_lv = (q * decay[lv * chunk:(lv + 1) * chunk]).astype(BF16)
                k_lv = (k * decay[(levels + lv) * chunk:(levels + lv + 1) * chunk]).astype(BF16)
                scores = jnp.where(level == lv, _dot_nt(q_lv, k_lv), scores)
            qd = (q * jnp.exp(b)).astype(BF16)
            s_t = s_scr[j]
            o = jnp.dot(scores.astype(BF16), v, preferred_element_type=F32) + _dot_nt(qd, s_t.astype(BF16))
            kdec = (k * jnp.exp(b_last - b)).astype(BF16)
            s_scr[j] = s_t * jnp.exp(b_last) + _dot_tn(v, kdec)
            ms = jnp.mean(o * o, axis=-1, keepdims=True)
            gg = g_ref[rows, vl]
            o_ref[rows, vl] = (o * lax.rsqrt(ms + NORM_EPS) * ng * (gg * jax.nn.sigmoid(gg))).astype(o_ref.dtype)

    @pl.when(c == n_steps - 1)
    def _():
        sT_ref[...] = s_scr[...]


def _gla(proj, row0, batch, seq, wa_pad, ba, norm_g, s0_t):
    chunk = min(GLA_CHUNK, seq)
    tb = _tile(seq, (256, 128, 64, 32, 16, 8))
    n_chunks = tb // chunk
    n_steps = seq // tb
    rb0 = row0 // tb
    hp = GLA_HEADS_PER_STEP
    prefix, level, levels = _gla_tables(chunk)

    def rows(b, c):
        return rb0 + b * n_steps + c

    fixed = lambda b, h, c: (0, 0)
    kern = functools.partial(_gla_kernel, chunk=chunk, levels=levels, n_chunks=n_chunks, n_steps=n_steps)
    return pl.pallas_call(
        kern,
        grid=(batch, GLA_HEADS // hp, n_steps),
        in_specs=[
            pl.BlockSpec((tb, hp * GLA_DK), lambda b, h, c: (rows(b, c), COL_GLA_Q // (hp * GLA_DK) + h)),
            pl.BlockSpec((tb, hp * GLA_DK), lambda b, h, c: (rows(b, c), COL_GLA_K // (hp * GLA_DK) + h)),
            pl.BlockSpec((tb, hp * GLA_DV), lambda b, h, c: (rows(b, c), COL_GLA_V // (hp * GLA_DV) + h)),
            pl.BlockSpec((tb, hp * GLA_DV), lambda b, h, c: (rows(b, c), COL_GLA_G // (hp * GLA_DV) + h)),
            pl.BlockSpec((tb, LANE), lambda b, h, c: (rows(b, c), COL_GLA_A // LANE)),
            pl.BlockSpec((LANE, hp * GLA_DK), lambda b, h, c: (0, h)),
            pl.BlockSpec((1, hp * GLA_DK), lambda b, h, c: (0, h)),
            pl.BlockSpec((1, GLA_DV), fixed),
            pl.BlockSpec(prefix.shape, fixed),
            pl.BlockSpec(level.shape, fixed),
            pl.BlockSpec((None, hp, GLA_DV, GLA_DK), lambda b, h, c: (b, h, 0, 0)),
        ],
        out_specs=[
            pl.BlockSpec((tb, hp * GLA_DV), lambda b, h, c: (b * n_steps + c, h)),
            pl.BlockSpec((None, hp, GLA_DV, GLA_DK), lambda b, h, c: (b, h, 0, 0)),
        ],
        out_shape=[
            jax.ShapeDtypeStruct((batch * seq, W_GLA), BF16),
            jax.ShapeDtypeStruct((batch, GLA_HEADS, GLA_DV, GLA_DK), F32),
        ],
        scratch_shapes=[pltpu.VMEM((hp, GLA_DV, GLA_DK), F32)],
        compiler_params=_params(("parallel", "parallel", "arbitrary"), VMEM_LIMIT),
        name="gla",
    )(proj, proj, proj, proj, proj, wa_pad, ba, norm_g, jnp.asarray(prefix, BF16), jnp.asarray(level), s0_t)


def _sb_suffix_and_total():
    r2 = lax.broadcasted_iota(jnp.int32, (SB_BLOCK, 2 * SB_BLOCK), 0)
    c2 = lax.broadcasted_iota(jnp.int32, (SB_BLOCK, 2 * SB_BLOCK), 1)
    return jnp.where((r2 > c2) | (c2 >= SB_BLOCK), 1.0, 0.0).astype(BF16)


def _sb_causal(n_rows):
    row = lax.broadcasted_iota(jnp.int32, (n_rows, SB_BLOCK), 0)
    col = lax.broadcasted_iota(jnp.int32, (n_rows, SB_BLOCK), 1)
    return col < row


def _head_lanes(j):
    return slice(j * SB_HEAD, (j + 1) * SB_HEAD)


def _sb_blocks(qs, key_values, runs, accs, suffix_and_total, causal):
    tq = qs[0].shape[0]
    log_betas, parts = [], []
    for q, (kblk, _) in zip(qs, key_values):
        z = _dot_nt(q, kblk) * SB_SCALE
        sp = jnp.maximum(z, 0.0) + jnp.log1p(jnp.exp(-jnp.abs(z)))
        log_1m = -sp
        if causal is not None:
            log_1m = jnp.where(causal, log_1m, 0.0)
        log_betas.append(z - sp)
        parts.extend(_split_bf16(log_1m))
    sums = jnp.dot(jnp.concatenate(parts, axis=0), suffix_and_total, preferred_element_type=F32)
    new_runs, new_accs = [], []
    for j, (_, vblk) in enumerate(key_values):
        cs = sums[2 * j * tq:(2 * j + 1) * tq] + sums[(2 * j + 1) * tq:(2 * j + 2) * tq]
        att = jnp.exp(log_betas[j] + cs[:, :SB_BLOCK] + runs[j])
        if causal is not None:
            att = jnp.where(causal, att, 0.0)
        new_accs.append(accs[j] + jnp.dot(att.astype(BF16), vblk, preferred_element_type=F32))
        new_runs.append(runs[j] + cs[:, SB_BLOCK:])
    return new_runs, new_accs


def _sb_sweep(qs, key_value_block, n_blocks, runs, accs, suffix_and_total):
    nh = len(qs)

    def run_max(runs):
        return functools.reduce(jnp.maximum, [jnp.max(r) for r in runs])

    def cond(carry):
        i, _, _, rmax = carry
        return jnp.logical_and(i < n_blocks, rmax > EXP_ZERO_BELOW)

    def body(carry):
        i, runs, accs, _ = carry
        rows = pl.ds(pl.multiple_of((n_blocks - 1 - i) * SB_BLOCK, SB_BLOCK), SB_BLOCK)
        runs, accs = _sb_blocks(qs, [key_value_block(rows, j) for j in range(nh)], runs, accs,
                                suffix_and_total, None)
        return i + 1, tuple(runs), tuple(accs), run_max(runs)

    _, _, accs, _ = lax.while_loop(cond, body, (jnp.int32(0), tuple(runs), tuple(accs), run_max(runs)))
    return accs


def _sb_finish(accs, g_ref, o_ref):
    for j, acc in enumerate(accs):
        ms = jnp.mean(acc * acc, axis=-1, keepdims=True)
        o_ref[:, _head_lanes(j)] = (acc * lax.rsqrt(ms + NORM_EPS) * g_ref[:, _head_lanes(j)]).astype(o_ref.dtype)


def _sb_prompt_kernel(q_ref, k_ref, v_ref, g_ref, o_ref):
    qi = pl.program_id(2)
    suffix_and_total = _sb_suffix_and_total()
    causal = _sb_causal(SB_BLOCK)
    zero = jnp.zeros((SB_BLOCK, SB_BLOCK), F32)

    def key_value_block(rows, j):
        return k_ref[rows, _head_lanes(j)].astype(BF16), v_ref[rows, _head_lanes(j)].astype(BF16)

    diag = pl.ds(pl.multiple_of(qi * SB_BLOCK, SB_BLOCK), SB_BLOCK)
    heads = range(SB_HEADS_PER_STEP)
    qs = [q_ref[:, _head_lanes(j)].astype(BF16) for j in heads]
    runs, accs = _sb_blocks(qs, [key_value_block(diag, j) for j in heads], [zero] * len(heads), [zero] * len(heads),
                            suffix_and_total, causal)
    _sb_finish(_sb_sweep(qs, key_value_block, qi, runs, accs, suffix_and_total), g_ref, o_ref)


def _sb_prompt(proj, batch, seq, norm_g):
    nq = seq // SB_BLOCK
    wide = SB_HEADS_PER_STEP * SB_HEAD
    return pl.pallas_call(
        _sb_prompt_kernel,
        grid=(batch, SB_HEADS // SB_HEADS_PER_STEP, nq),
        in_specs=[
            pl.BlockSpec((SB_BLOCK, wide), lambda b, h, i: (b * nq + i, COL_SB_Q // wide + h)),
            pl.BlockSpec((seq, wide), lambda b, h, i: (b, COL_SB_K // wide + h)),
            pl.BlockSpec((seq, wide), lambda b, h, i: (b, COL_SB_V // wide + h)),
            pl.BlockSpec((1, wide), lambda b, h, i: (0, h)),
        ],
        out_specs=pl.BlockSpec((SB_BLOCK, wide), lambda b, h, i: (b * nq + i, h)),
        out_shape=jax.ShapeDtypeStruct((batch * seq, W_SB), BF16),
        compiler_params=_params(("parallel", "parallel", "arbitrary"), VMEM_LIMIT),
        name="sb_prompt",
    )(proj, proj, proj, norm_g)


def _sb_sample_kernel(q_ref, k_ref, v_ref, kp_ref, vp_ref, g_ref, o_ref, *, seq, n_past_blocks):
    suffix_and_total = _sb_suffix_and_total()
    causal = _sb_causal(seq)
    pad = jnp.zeros((SB_BLOCK - seq, SB_HEAD), BF16)
    zero = jnp.zeros((seq, SB_BLOCK), F32)

    def past_block(rows, j):
        return kp_ref[rows, j, :].astype(BF16), vp_ref[rows, j, :].astype(BF16)

    def new_block(j):
        return (jnp.concatenate([k_ref[:, _head_lanes(j)].astype(BF16), pad], axis=0),
                jnp.concatenate([v_ref[:, _head_lanes(j)].astype(BF16), pad], axis=0))

    heads = range(SB_HEADS)
    qs = [q_ref[:, _head_lanes(j)].astype(BF16) for j in heads]
    runs, accs = _sb_blocks(qs, [new_block(j) for j in heads], [zero] * len(heads), [zero] * len(heads),
                            suffix_and_total, causal)
    _sb_finish(_sb_sweep(qs, past_block, n_past_blocks, runs, accs, suffix_and_total), g_ref, o_ref)


def _sb_sample(proj, row0, batch, seq, layer, cache_k, cache_v, norm_g):
    past = cache_k.shape[2]
    rb0 = row0 // seq
    kern = functools.partial(_sb_sample_kernel, seq=seq, n_past_blocks=past // SB_BLOCK)
    cache_spec = pl.BlockSpec((None, None, past, SB_HEADS, SB_HEAD), lambda b: (layer, b, 0, 0, 0))
    return pl.pallas_call(
        kern,
        grid=(batch,),
        in_specs=[
            pl.BlockSpec((seq, W_SB), lambda b: (rb0 + b, COL_SB_Q // W_SB)),
            pl.BlockSpec((seq, W_SB), lambda b: (rb0 + b, COL_SB_K // W_SB)),
            pl.BlockSpec((seq, W_SB), lambda b: (rb0 + b, COL_SB_V // W_SB)),
            cache_spec, cache_spec,
            pl.BlockSpec((1, W_SB), lambda b: (0, 0)),
        ],
        out_specs=pl.BlockSpec((seq, W_SB), lambda b: (b, 0)),
        out_shape=jax.ShapeDtypeStruct((batch * seq, W_SB), BF16),
        compiler_params=_params(("parallel",), VMEM_LIMIT),
        name="sb_sample",
    )(proj, proj, proj, cache_k, cache_v, norm_g)


def _rwkv_prep_kernel(p_ref, shift_ref, mu_ref, w0_ref, a0_ref, w2_ref, a2_ref, g2_ref,
                      r_out, w_out, k_out, v_out, a_out, g_out, last_out, prev_scr, *, tiles_per_seq):
    @pl.when(pl.program_id(0) % tiles_per_seq == 0)
    def _():
        prev_scr[...] = shift_ref[...]

    p = p_ref[...]
    first = lax.broadcasted_iota(jnp.int32, p.shape, 0) == 0
    prev = jnp.where(first, prev_scr[...], pltpu.roll(p, 1, 0))
    last = p[p.shape[0] - 1:, :]
    prev_scr[...] = last
    last_out[...] = last
    pm = p + (prev - p) * mu_ref[...]
    o = 0
    r = pm[:, o:o + W_RWKV]; o += W_RWKV
    w_low = pm[:, o:o + D_LORA]; o += D_LORA
    k = pm[:, o:o + W_RWKV]; o += W_RWKV
    v = pm[:, o:o + W_RWKV]; o += W_RWKV
    a_low = pm[:, o:o + D_LORA]; o += D_LORA
    g_low = pm[:, o:o + D_GATE_PAD]
    log_w = -math.exp(-0.5) * jax.nn.sigmoid(
        w0_ref[...] + jnp.dot(jnp.tanh(w_low).astype(BF16), w2_ref[...], preferred_element_type=F32))
    a = jax.nn.sigmoid(a0_ref[...] + jnp.dot(a_low.astype(BF16), a2_ref[...], preferred_element_type=F32))
    g = jnp.dot(jax.nn.sigmoid(g_low).astype(BF16), g2_ref[...], preferred_element_type=F32)
    r_out[...] = r
    w_out[...] = jnp.exp(log_w)
    k_out[...] = k
    v_out[...] = v
    a_out[...] = a
    g_out[...] = g


def _rwkv_prep(proj, row0, batch, seq, shift, mu, w0, a0, w2, a2, g2):
    tm = _tile(seq, (128, 64, 32, 16, 8))
    nt = seq // tm
    m = batch * seq
    rb0 = row0 // tm
    row = lambda i: (i, 0)
    fixed = lambda i: (0, 0)
    per_seq = pl.BlockSpec((None, 1, P_RWKV_PAD), lambda i: (i // nt, 0, 0))
    vec = pl.BlockSpec((1, W_RWKV), fixed)
    out = pl.BlockSpec((tm, W_RWKV), row)
    return pl.pallas_call(
        functools.partial(_rwkv_prep_kernel, tiles_per_seq=nt),
        grid=(m // tm,),
        in_specs=[pl.BlockSpec((tm, P_RWKV_PAD), lambda i: (rb0 + i, 0)), per_seq,
                  pl.BlockSpec((1, P_RWKV_PAD), fixed), vec, vec,
                  pl.BlockSpec((D_LORA, W_RWKV), fixed), pl.BlockSpec((D_LORA, W_RWKV), fixed),
                  pl.BlockSpec((D_GATE_PAD, W_RWKV), fixed)],
        out_specs=[out] * 6 + [per_seq],
        out_shape=[jax.ShapeDtypeStruct((m, W_RWKV), F32)] * 6
        + [jax.ShapeDtypeStruct((batch, 1, P_RWKV_PAD), F32)],
        scratch_shapes=[pltpu.VMEM((1, P_RWKV_PAD), F32)],
        compiler_params=_params(("arbitrary",), VMEM_LIMIT),
        name="rwkv_prep",
    )(proj, shift, mu, w0, a0, w2, a2, g2)


def _spread_lanes(copies):
    lanes = copies[0].shape[-1]
    full = lanes // LANE
    pieces = [c[..., i * LANE:(i + 1) * LANE] for i in range(full) for c in copies]
    if lanes % LANE:
        pieces += [c[..., full * LANE:] for c in copies]
    return pieces[0] if len(pieces) == 1 else jnp.concatenate(pieces, axis=-1)


def _gather_lanes(x, n_copies):
    lanes = x.shape[-1] // n_copies
    full, rem = lanes // LANE, lanes % LANE
    out = []
    for g in range(n_copies):
        pieces = [x[..., (i * n_copies + g) * LANE:(i * n_copies + g + 1) * LANE] for i in range(full)]
        if rem:
            start = full * n_copies * LANE + g * rem
            pieces.append(x[..., start:start + rem])
        out.append(pieces[0] if len(pieces) == 1 else jnp.concatenate(pieces, axis=-1))
    return out


def _rwkv_scan_kernel(r_ref, w_ref, k_ref, v_ref, a_ref, rk_ref, lng_ref, lnb_ref, kkp_ref, kap_ref, s0_ref,
                      z_ref, sT_ref, s_scr, kk_scr, b_scr, w_scr, k_scr, r_scr, v_scr, y_scr,
                      *, t_block, n_steps, vs):
    c = pl.program_id(1)
    n = RWKV_HEAD
    nv = n // vs
    lanes_in = r_ref.shape[-1]
    lanes_work = lanes_in * vs

    def dup(x):
        return _spread_lanes([x] * vs)

    @pl.when(c == 0)
    def _():
        s_scr[...] = s0_ref[...]

    def step(t, carry):
        k_raw = k_ref[t]
        a = a_ref[t]
        kk_raw = k_raw * kkp_ref[...]
        norm = jnp.sqrt(jnp.sum(kk_raw * kk_raw, axis=0, keepdims=True))
        kk = kk_raw / jnp.maximum(norm, 1e-12)
        k_t = k_raw * (1.0 + (a - 1.0) * kap_ref[...])
        kk_scr[...] = dup(kk)
        b_scr[...] = dup(kk * a)
        w_scr[...] = dup(w_ref[t])
        k_scr[...] = dup(k_t)
        r_scr[...] = dup(r_ref[t])
        v_t = v_ref[t]
        v_scr[...] = _spread_lanes([v_t[g * nv:(g + 1) * nv] for g in range(vs)])

        for lg in range(lanes_work // LANE):
            ls = pl.ds(lg * LANE, LANE)

            def value_rows(vo, carry2):
                v0 = pl.multiple_of(vo * SUBLANE, SUBLANE)
                v_tile = v_scr[pl.ds(v0, SUBLANE), ls]
                y_rows = []
                for vi in range(SUBLANE):
                    s_v = s_scr[v0 + vi, :, ls]
                    sa = jnp.sum(s_v * kk_scr[:, ls], axis=0, keepdims=True)
                    s_new = s_v * w_scr[:, ls] - sa * b_scr[:, ls] + v_tile[vi:vi + 1, :] * k_scr[:, ls]
                    s_scr[v0 + vi, :, ls] = s_new
                    y_rows.append(jnp.sum(s_new * r_scr[:, ls], axis=0, keepdims=True))
                y_scr[pl.ds(v0, SUBLANE), ls] = jnp.concatenate(y_rows, axis=0)
                return carry2

            lax.fori_loop(0, nv // SUBLANE, value_rows, 0)

        y_groups = _gather_lanes(y_scr[...], vs)
        y = y_groups[0] if vs == 1 else jnp.concatenate(y_groups, axis=0)
        mean = jnp.mean(y, axis=0, keepdims=True)
        d = y - mean
        var = jnp.mean(d * d, axis=0, keepdims=True)
        bonus = jnp.sum(r_ref[t] * k_t * rk_ref[...], axis=0, keepdims=True) * v_t
        z_ref[t] = d * lax.rsqrt(var + RWKV_GN_EPS) * lng_ref[...] + lnb_ref[...] + bonus
        return carry

    lax.fori_loop(0, t_block, step, 0)

    @pl.when(c == n_steps - 1)
    def _():
        sT_ref[...] = s_scr[...]


def _rwkv_scan(r, w, k, v, a, rk, lng, lnb, kkp, kap, s0, vs):
    seq, n, lanes = r.shape
    nv = n // vs
    t_block = _tile(seq, (8, 4, 2, 1))
    n_steps = seq // t_block
    lb = lanes if vs > 1 else _tile(lanes, (3 * LANE, 2 * LANE, LANE))
    lbw = lb * vs
    seq_spec = pl.BlockSpec((t_block, n, lb), lambda g, c: (c, 0, g))
    vec_spec = pl.BlockSpec((n, lb), lambda g, c: (0, g))
    st_spec = pl.BlockSpec((nv, n, lbw), lambda g, c: (0, 0, g))
    kern = functools.partial(_rwkv_scan_kernel, t_block=t_block, n_steps=n_steps, vs=vs)
    work = lambda rows: pltpu.VMEM((rows, lbw), F32)
    return pl.pallas_call(
        kern,
        grid=(lanes // lb, n_steps),
        in_specs=[seq_spec] * 5 + [vec_spec] * 5 + [st_spec],
        out_specs=[seq_spec, st_spec],
        out_shape=[jax.ShapeDtypeStruct((seq, n, lanes), F32), jax.ShapeDtypeStruct((nv, n, lanes * vs), F32)],
        scratch_shapes=[pltpu.VMEM((nv, n, lbw), F32), work(n), work(n), work(n), work(n), work(n),
                        work(nv), work(nv)],
        compiler_params=_params(("parallel", "arbitrary"), VMEM_LIMIT),
        name="rwkv_scan",
    )(r, w, k, v, a, rk, lng, lnb, kkp, kap, s0)


def _merge_kernel(gla_p, z_p, g_p, sb_p, gla_s, z_s, g_s, sb_s, o_ref, *, n_prompt_tiles):
    def write(gla_ref, z_ref, g_ref, sb_ref):
        o_ref[:, 0:W_GLA] = gla_ref[...]
        o_ref[:, W_GLA:W_GLA + W_RWKV] = (z_ref[...] * g_ref[...]).astype(o_ref.dtype)
        o_ref[:, W_GLA + W_RWKV:D_MODEL] = sb_ref[...]

    i = pl.program_id(0)

    @pl.when(i < n_prompt_tiles)
    def _():
        write(gla_p, z_p, g_p, sb_p)

    @pl.when(i >= n_prompt_tiles)
    def _():
        write(gla_s, z_s, g_s, sb_s)


def _merge(prompt, sample):
    mp, ms = prompt[1].shape[0], sample[1].shape[0]
    tm = _tile(math.gcd(mp, ms), (512, 256, 128, 64, 32, 16))
    n_p, n_s = mp // tm, ms // tm
    widths = (W_GLA, W_RWKV, W_RWKV, W_SB)
    p_specs = [pl.BlockSpec((tm, wd), lambda i: (jnp.minimum(i, n_p - 1), 0)) for wd in widths]
    s_specs = [pl.BlockSpec((tm, wd), lambda i: (jnp.maximum(i - n_p, 0), 0)) for wd in widths]
    return pl.pallas_call(
        functools.partial(_merge_kernel, n_prompt_tiles=n_p),
        grid=(n_p + n_s,),
        in_specs=p_specs + s_specs,
        out_specs=pl.BlockSpec((tm, D_MODEL), lambda i: (i, 0)),
        out_shape=jax.ShapeDtypeStruct((mp + ms, D_MODEL), BF16),
        compiler_params=_params(("parallel",), VMEM_LIMIT),
        name="merge",
    )(*prompt, *sample)


def _to_scan_layout(x, batch, seq):
    x = x.reshape(batch, seq, RWKV_HEADS, RWKV_HEAD).transpose(1, 3, 0, 2)
    return x.reshape(seq, RWKV_HEAD, batch * RWKV_HEADS)


def _from_scan_layout(z, batch, seq):
    z = z.reshape(seq, RWKV_HEAD, batch, RWKV_HEADS)
    return z.transpose(2, 0, 3, 1).reshape(batch * seq, W_RWKV)


def _head_vec_to_scan_layout(x, batch):
    x = x.reshape(RWKV_HEADS, RWKV_HEAD).T
    return jnp.tile(x[:, None, :], (1, batch, 1)).reshape(RWKV_HEAD, batch * RWKV_HEADS)


def _rwkv_group(ins, batch, seq, params, s0):
    lanes = batch * RWKV_HEADS
    n = RWKV_HEAD
    vs = next(f for f in (1, 2, 4, 8) if (lanes * f) % LANE == 0)
    nv = n // vs
    s0_l = s0.transpose(2, 3, 0, 1).reshape(n, n, lanes)
    s0_l = _spread_lanes([s0_l[g * nv:(g + 1) * nv] for g in range(vs)])
    vecs = [_head_vec_to_scan_layout(x, batch) for x in params]
    z, s_t = _rwkv_scan(*ins, *vecs, s0_l, vs)
    s_new = jnp.concatenate(_gather_lanes(s_t, vs), axis=0).reshape(n, n, batch, RWKV_HEADS).transpose(2, 3, 0, 1)
    return _from_scan_layout(z, batch, seq), s_new


W_IN_SEGMENTS = (
    (COL_RWKV, P_GLA, P_RWKV, P_RWKV_PAD - P_RWKV),
    (COL_GLA_Q, 0, GLA_QK, 0),
    (COL_SB_Q, P_GLA + P_RWKV, P_SB, 0),
    (COL_GLA_V, 2 * GLA_QK, W_GLA, 0),
    (COL_GLA_G, 2 * GLA_QK + W_GLA, W_GLA, 0),
    (COL_GLA_K, GLA_QK, GLA_QK, 0),
    (COL_GLA_A, 2 * GLA_QK + 2 * W_GLA, GLA_LOWRANK, P_PAD - COL_GLA_A - GLA_LOWRANK),
)


def _rearrange_w_in_kernel(w_ref, o_ref):
    rows = w_ref.shape[0]
    for dst, src, width, pad in W_IN_SEGMENTS:
        piece = w_ref[:, src:src + width].astype(BF16)
        if pad:
            piece = jnp.concatenate([piece, jnp.zeros((rows, pad), BF16)], axis=1)
        o_ref[:, dst:dst + width + pad] = piece


def _rearrange_w_in(w_in, layer):
    _, d, p = w_in.shape
    tk = _tile(d, (64, 32, 16))
    return pl.pallas_call(
        _rearrange_w_in_kernel,
        grid=(d // tk,),
        in_specs=[pl.BlockSpec((None, tk, p), lambda i: (layer, i, 0))],
        out_specs=pl.BlockSpec((tk, P_PAD), lambda i: (i, 0)),
        out_shape=jax.ShapeDtypeStruct((d, P_PAD), BF16),
        compiler_params=_params(("parallel",), VMEM_LIMIT),
        name="rearrange_w_in",
    )(w_in)


def _cast_pad_cols_kernel(w_ref, o_ref):
    rows, n = w_ref.shape
    pad = o_ref.shape[1] - n
    o_ref[:, :n] = w_ref[...].astype(BF16)
    if pad:
        o_ref[:, n:] = jnp.zeros((rows, pad), BF16)


def _cast_pad_cols(w, layer, n_out):
    _, k, n = w.shape
    tk = _tile(k, (128, 64, 32, 16))
    return pl.pallas_call(
        _cast_pad_cols_kernel,
        grid=(k // tk,),
        in_specs=[pl.BlockSpec((None, tk, n), lambda i: (layer, i, 0))],
        out_specs=pl.BlockSpec((tk, n_out), lambda i: (i, 0)),
        out_shape=jax.ShapeDtypeStruct((k, n_out), BF16),
        compiler_params=_params(("parallel",), VMEM_LIMIT),
        name="cast_pad_cols",
    )(w)


def _cast_pad_rows_kernel(w_ref, o_ref, *, n_data_tiles):
    @pl.when(pl.program_id(0) < n_data_tiles)
    def _():
        o_ref[...] = w_ref[...].astype(BF16)

    @pl.when(pl.program_id(0) >= n_data_tiles)
    def _():
        o_ref[...] = jnp.zeros(o_ref.shape, BF16)


def _cast_pad_rows(w, layer, k_out):
    _, k, n = w.shape
    tk = math.gcd(k, k_out - k)
    n_data = k // tk
    return pl.pallas_call(
        functools.partial(_cast_pad_rows_kernel, n_data_tiles=n_data),
        grid=(k_out // tk,),
        in_specs=[pl.BlockSpec((None, tk, n), lambda i: (layer, jnp.minimum(i, n_data - 1), 0))],
        out_specs=pl.BlockSpec((tk, n), lambda i: (i, 0)),
        out_shape=jax.ShapeDtypeStruct((k_out, n), BF16),
        compiler_params=_params(("parallel",), VMEM_LIMIT),
        name="cast_pad_rows",
    )(w)


def _heads_out_kernel(x_ref, o_ref):
    for h in range(SB_HEADS):
        o_ref[:, h, :] = x_ref[:, _head_lanes(h)]


def _heads_out(proj, col0, row0, rows):
    tm = _tile(rows, (512, 256, 128, 64, 32, 16, 8))
    rb0 = row0 // tm
    return pl.pallas_call(
        _heads_out_kernel,
        grid=(rows // tm,),
        in_specs=[pl.BlockSpec((tm, W_SB), lambda i: (rb0 + i, col0 // W_SB))],
        out_specs=pl.BlockSpec((tm, SB_HEADS, SB_HEAD), lambda i: (i, 0, 0)),
        out_shape=jax.ShapeDtypeStruct((rows, SB_HEADS, SB_HEAD), F32),
        compiler_params=_params(("parallel",), VMEM_LIMIT),
        name="heads_out",
    )(proj)


def kernel(x_prompt, x_sample, state_gla, state_rwkv, state_rwkv_shift, cache_sb_k, cache_sb_v, norm1_g, w_in, gla_wa2, gla_ba, gla_norm_g, rwkv_mu, rwkv_w0, rwkv_w2, rwkv_a0, rwkv_a2, rwkv_g2, rwkv_k_k, rwkv_k_a, rwkv_r_k, rwkv_ln_g, rwkv_ln_b, sb_norm_g, w_out, norm2_g, w_gate, w_up, w_down, final_g):
    bp, tp, d = x_prompt.shape
    bs, ts, _ = x_sample.shape
    depth = w_in.shape[0]
    mp, ms = bp * tp, bs * ts
    x_parts = (x_prompt.reshape(mp, d), x_sample.reshape(ms, d))
    x = None

    outs = {name: [] for name in ("gla_p", "gla_s", "rw_p", "rw_s", "sh_p", "sh_s", "k_p", "k_s", "v_p", "v_s")}
    row1 = lambda a: a.reshape(1, -1)
    for l in range(depth):
        h = _rmsnorm_joined(*x_parts, norm1_g[l], BF16) if x is None else _rmsnorm(x, norm1_g[l], BF16)
        proj = _matmul(h, _rearrange_w_in(w_in, l), F32)
        for nm, c0 in (("k", COL_SB_K), ("v", COL_SB_V)):
            outs[nm + "_p"].append(_heads_out(proj, c0, 0, mp).reshape(bp, tp, SB_HEADS, SB_HEAD))
            outs[nm + "_s"].append(_heads_out(proj, c0, mp, ms).reshape(bs, ts, SB_HEADS, SB_HEAD))

        lane_pad = (0, P_RWKV_PAD - P_RWKV)
        prep_args = (row1(jnp.pad(rwkv_mu[l], lane_pad)), row1(rwkv_w0[l]), row1(rwkv_a0[l]),
                     rwkv_w2[l].astype(BF16), rwkv_a2[l].astype(BF16),
                     jnp.pad(rwkv_g2[l], ((0, D_GATE_PAD - D_GATE_LORA), (0, 0))).astype(BF16))
        vec_args = (rwkv_r_k[l].reshape(-1), rwkv_ln_g[l], rwkv_ln_b[l], rwkv_k_k[l], rwkv_k_a[l])
        *seqs_p, gate_p, last_p = _rwkv_prep(proj, 0, bp, tp, jnp.zeros((bp, 1, P_RWKV_PAD), F32), *prep_args)
        *seqs_s, gate_s, last_s = _rwkv_prep(proj, mp, bs, ts, jnp.pad(state_rwkv_shift[l], ((0, 0), (0, 0), lane_pad)),
                                             *prep_args)
        outs["sh_p"].append(last_p[:, :, :P_RWKV])
        outs["sh_s"].append(last_s[:, :, :P_RWKV])
        scan_in_p = [_to_scan_layout(s, bp, tp) for s in seqs_p]
        scan_in_s = [_to_scan_layout(s, bs, ts) for s in seqs_s]

        wa_pad = jnp.pad(gla_wa2[l], ((0, LANE - GLA_LOWRANK), (0, 0))).astype(BF16)
        gla_args = (wa_pad, row1(gla_ba[l]), row1(gla_norm_g[l]))
        o_gla_p, sg_p = _gla(proj, 0, bp, tp, *gla_args, jnp.zeros((bp, GLA_HEADS, GLA_DV, GLA_DK), F32))
        o_gla_s, sg_s = _gla(proj, mp, bs, ts, *gla_args, state_gla[l].swapaxes(-1, -2))
        outs["gla_p"].append(sg_p.swapaxes(-1, -2))
        outs["gla_s"].append(sg_s.swapaxes(-1, -2))

        sbg = row1(sb_norm_g[l])
        o_sb_p = _sb_prompt(proj, bp, tp, sbg)
        o_sb_s = _sb_sample(proj, mp, bs, ts, l, cache_sb_k, cache_sb_v, sbg)

        z_p, sr_p = _rwkv_group(scan_in_p, bp, tp, vec_args, jnp.zeros((bp, RWKV_HEADS, RWKV_HEAD, RWKV_HEAD), F32))
        z_s, sr_s = _rwkv_group(scan_in_s, bs, ts, vec_args, state_rwkv[l])
        outs["rw_p"].append(sr_p)
        outs["rw_s"].append(sr_s)

        merged = _merge((o_gla_p, z_p, gate_p, o_sb_p), (o_gla_s, z_s, gate_s, o_sb_s))
        w_o = w_out[l].astype(BF16)
        x = _matmul_residual_joined(merged, w_o, *x_parts) if x is None else _matmul_residual(merged, w_o, x, tk=d)

        h2 = _rmsnorm(x, norm2_g[l], BF16)
        act = _swiglu(h2, _cast_pad_cols(w_gate, l, D_FF_PAD), _cast_pad_cols(w_up, l, D_FF_PAD))
        x = _matmul_residual(act, _cast_pad_rows(w_down, l, D_FF_PAD), x, tk=D_FF_PAD // 4)

    y_p, y_s = _rmsnorm_split(x, final_g, mp)
    sd = state_gla.dtype
    st = lambda name, dt=None: jnp.stack(outs[name]) if dt is None else jnp.stack(outs[name]).astype(dt)
    return (y_p.reshape(bp, tp, d), y_s.reshape(bs, ts, d),
            st("gla_p", sd), st("gla_s", sd), st("rw_p", sd), st("rw_s", sd),
            st("sh_p"), st("sh_s"), st("k_p"), st("k_s"), st("v_p"), st("v_s"))
```

```python
import functools
import math

import jax
import jax.numpy as jnp
import numpy as np
from jax import lax
from jax.experimental import pallas as pl
from jax.experimental.pallas import tpu as pltpu

F32 = jnp.float32
BF16 = jnp.bfloat16

D_MODEL = 4096
NORM_EPS = 1e-5
W_GLA = 1536
W_RWKV = 1536
W_SB = 1024
GLA_HEADS = 6
GLA_DV = 256
GLA_DK = 128
GLA_QK = 768
GLA_LOWRANK = 16
GLA_TAU = 16.0
GLA_CHUNK = 64
GLA_HEADS_PER_STEP = 3
RWKV_HEAD = 64
RWKV_HEADS = 24
D_LORA = 128
D_GATE_LORA = 480
D_GATE_PAD = 512
RWKV_GN_EPS = 64e-5
SB_HEAD = 128
SB_HEADS = 8
SB_SCALE = SB_HEAD ** -0.5
SB_BLOCK = 128
SB_HEADS_PER_STEP = 8
D_FF = 11008
D_FF_PAD = 11264
P_GLA = 4624
P_RWKV = 5344
P_RWKV_PAD = 5376
P_SB = 3072

COL_RWKV = 0
COL_GLA_Q = P_RWKV_PAD
COL_SB_Q = COL_GLA_Q + GLA_QK
COL_SB_K = COL_SB_Q + W_SB
COL_SB_V = COL_SB_K + W_SB
COL_GLA_V = COL_SB_V + W_SB
COL_GLA_G = COL_GLA_V + W_GLA
COL_GLA_K = COL_GLA_G + W_GLA
COL_GLA_A = COL_GLA_K + GLA_QK
P_PAD = 13312

LANE = 128
SUBLANE = 8
VMEM_LIMIT = 56 * 1024 * 1024

EXP_ZERO_BELOW = -104.0


def _tile(n, candidates):
    for c in candidates:
        if n % c == 0:
            return c
    return n


def _params(sem, vmem=None):
    return pltpu.CompilerParams(dimension_semantics=sem, vmem_limit_bytes=vmem)


def _rmsnorm_kernel(x_ref, g_ref, o_ref):
    x = x_ref[...]
    ms = jnp.mean(x * x, axis=-1, keepdims=True)
    o_ref[...] = (x * lax.rsqrt(ms + NORM_EPS) * g_ref[...]).astype(o_ref.dtype)


def _rmsnorm(x, g, out_dtype):
    m, d = x.shape
    tm = _tile(m, (512, 256, 128, 64, 32, 16, 8))
    return pl.pallas_call(
        _rmsnorm_kernel,
        grid=(m // tm,),
        in_specs=[pl.BlockSpec((tm, d), lambda i: (i, 0)), pl.BlockSpec((1, d), lambda i: (0, 0))],
        out_specs=pl.BlockSpec((tm, d), lambda i: (i, 0)),
        out_shape=jax.ShapeDtypeStruct((m, d), out_dtype),
        compiler_params=_params(("parallel",), VMEM_LIMIT),
        name="rmsnorm",
    )(x, g.reshape(1, d))


def _two_part_specs(tm, width, n_first, col=None):
    col = col or (lambda *idx: 0)
    return [pl.BlockSpec((tm, width), lambda *idx: (jnp.minimum(idx[0], n_first - 1), col(*idx))),
            pl.BlockSpec((tm, width), lambda *idx: (jnp.maximum(idx[0] - n_first, 0), col(*idx)))]


def _rmsnorm_joined_kernel(a_ref, b_ref, g_ref, o_ref, *, n_first):
    def run(x_ref):
        _rmsnorm_kernel(x_ref, g_ref, o_ref)

    @pl.when(pl.program_id(0) < n_first)
    def _():
        run(a_ref)

    @pl.when(pl.program_id(0) >= n_first)
    def _():
        run(b_ref)


def _rmsnorm_joined(a, b, g, out_dtype):
    (ma, d), mb = a.shape, b.shape[0]
    tm = _tile(math.gcd(ma, mb), (512, 256, 128, 64, 32, 16, 8))
    n_first = ma // tm
    return pl.pallas_call(
        functools.partial(_rmsnorm_joined_kernel, n_first=n_first),
        grid=((ma + mb) // tm,),
        in_specs=_two_part_specs(tm, d, n_first) + [pl.BlockSpec((1, d), lambda i: (0, 0))],
        out_specs=pl.BlockSpec((tm, d), lambda i: (i, 0)),
        out_shape=jax.ShapeDtypeStruct((ma + mb, d), out_dtype),
        compiler_params=_params(("parallel",), VMEM_LIMIT),
        name="rmsnorm_joined",
    )(a, b, g.reshape(1, d))


def _rmsnorm_split_kernel(x_ref, g_ref, a_ref, b_ref, *, n_first):
    x = x_ref[...]
    ms = jnp.mean(x * x, axis=-1, keepdims=True)
    y = x * lax.rsqrt(ms + NORM_EPS) * g_ref[...]
    i = pl.program_id(0)

    @pl.when(i < n_first)
    def _():
        a_ref[...] = y

    @pl.when(i >= n_first)
    def _():
        b_ref[...] = y


def _rmsnorm_split(x, g, m_first):
    m, d = x.shape
    tm = _tile(math.gcd(m_first, m - m_first), (512, 256, 128, 64, 32, 16, 8))
    n_first = m_first // tm
    return pl.pallas_call(
        functools.partial(_rmsnorm_split_kernel, n_first=n_first),
        grid=(m // tm,),
        in_specs=[pl.BlockSpec((tm, d), lambda i: (i, 0)), pl.BlockSpec((1, d), lambda i: (0, 0))],
        out_specs=[pl.BlockSpec((tm, d), lambda i: (jnp.minimum(i, n_first - 1), 0)),
                   pl.BlockSpec((tm, d), lambda i: (jnp.maximum(i - n_first, 0), 0))],
        out_shape=[jax.ShapeDtypeStruct((m_first, d), F32), jax.ShapeDtypeStruct((m - m_first, d), F32)],
        compiler_params=_params(("arbitrary",), VMEM_LIMIT),
        name="rmsnorm_split",
    )(x, g.reshape(1, d))


def _mm_kernel(x_ref, w_ref, o_ref):
    o_ref[...] = jnp.dot(x_ref[...], w_ref[...], preferred_element_type=F32).astype(o_ref.dtype)


def _mm_res_kernel(x_ref, w_ref, r_ref, o_ref):
    acc = jnp.dot(x_ref[...], w_ref[...], preferred_element_type=F32)
    k = pl.program_id(2)

    @pl.when(k == 0)
    def _():
        o_ref[...] = r_ref[...] + acc

    @pl.when(k != 0)
    def _():
        o_ref[...] += acc


def _swiglu_kernel(x_ref, wg_ref, wu_ref, o_ref):
    x = x_ref[...]
    g = jnp.dot(x, wg_ref[...], preferred_element_type=F32)
    u = jnp.dot(x, wu_ref[...], preferred_element_type=F32)
    o_ref[...] = (g * jax.nn.sigmoid(g) * u).astype(o_ref.dtype)


def _matmul(x, w, out_dtype):
    m, k = x.shape
    n = w.shape[1]
    tm = _tile(m, (1024, 512, 256, 128, 64, 32, 16, 8))
    tn = _tile(n, (1024, 512, 256, 128))
    return pl.pallas_call(
        _mm_kernel,
        grid=(m // tm, n // tn),
        in_specs=[pl.BlockSpec((tm, k), lambda i, j: (i, 0)), pl.BlockSpec((k, tn), lambda i, j: (0, j))],
        out_specs=pl.BlockSpec((tm, tn), lambda i, j: (i, j)),
        out_shape=jax.ShapeDtypeStruct((m, n), out_dtype),
        compiler_params=_params(("parallel", "parallel"), VMEM_LIMIT),
        name="matmul",
    )(x, w)


def _matmul_residual(x, w, res, tk):
    m, k = x.shape
    n = w.shape[1]
    tm = _tile(m, (1024, 512, 256, 128, 64, 32, 16, 8))
    tn = _tile(n, (1024, 512, 256, 128))
    return pl.pallas_call(
        _mm_res_kernel,
        grid=(m // tm, n // tn, k // tk),
        in_specs=[pl.BlockSpec((tm, tk), lambda i, j, kk: (i, kk)),
                  pl.BlockSpec((tk, tn), lambda i, j, kk: (kk, j)),
                  pl.BlockSpec((tm, tn), lambda i, j, kk: (i, j))],
        out_specs=pl.BlockSpec((tm, tn), lambda i, j, kk: (i, j)),
        out_shape=jax.ShapeDtypeStruct((m, n), F32),
        compiler_params=_params(("parallel", "parallel", "arbitrary"), VMEM_LIMIT),
        name="matmul_residual",
    )(x, w, res)


def _mm_res_joined_kernel(x_ref, w_ref, ra_ref, rb_ref, o_ref, *, n_first):
    acc = jnp.dot(x_ref[...], w_ref[...], preferred_element_type=F32)

    @pl.when(pl.program_id(0) < n_first)
    def _():
        o_ref[...] = ra_ref[...] + acc

    @pl.when(pl.program_id(0) >= n_first)
    def _():
        o_ref[...] = rb_ref[...] + acc


def _matmul_residual_joined(x, w, res_a, res_b):
    m, k = x.shape
    n = w.shape[1]
    ma = res_a.shape[0]
    tm = _tile(math.gcd(ma, m - ma), (1024, 512, 256, 128, 64, 32, 16, 8))
    tn = _tile(n, (512, 256, 128))
    n_first = ma // tm
    return pl.pallas_call(
        functools.partial(_mm_res_joined_kernel, n_first=n_first),
        grid=(m // tm, n // tn),
        in_specs=[pl.BlockSpec((tm, k), lambda i, j: (i, 0)), pl.BlockSpec((k, tn), lambda i, j: (0, j))]
        + _two_part_specs(tm, tn, n_first, col=lambda i, j: j),
        out_specs=pl.BlockSpec((tm, tn), lambda i, j: (i, j)),
        out_shape=jax.ShapeDtypeStruct((m, n), F32),
        compiler_params=_params(("parallel", "parallel"), VMEM_LIMIT),
        name="matmul_residual_joined",
    )(x, w, res_a, res_b)


def _swiglu(x, wg, wu):
    m, k = x.shape
    n = wg.shape[1]
    tm = _tile(m, (1024, 512, 256, 128, 64, 32, 16, 8))
    tn = _tile(n, (512, 256, 128))
    return pl.pallas_call(
        _swiglu_kernel,
        grid=(m // tm, n // tn),
        in_specs=[pl.BlockSpec((tm, k), lambda i, j: (i, 0)),
                  pl.BlockSpec((k, tn), lambda i, j: (0, j)),
                  pl.BlockSpec((k, tn), lambda i, j: (0, j))],
        out_specs=pl.BlockSpec((tm, tn), lambda i, j: (i, j)),
        out_shape=jax.ShapeDtypeStruct((m, n), BF16),
        compiler_params=_params(("parallel", "parallel"), VMEM_LIMIT),
        name="swiglu",
    )(x, wg, wu)


def _log_sigmoid(x):
    return jnp.minimum(x, 0.0) - jnp.log1p(jnp.exp(-jnp.abs(x)))


def _split_bf16(x):
    hi = x.astype(BF16)
    lo = (x - hi.astype(F32)).astype(BF16)
    return hi, lo


def _dot_nt(a, b):
    return lax.dot_general(a, b, (((1,), (1,)), ((), ())), preferred_element_type=F32)


def _dot_tn(a, b):
    return lax.dot_general(a, b, (((0,), (0,)), ((), ())), preferred_element_type=F32)


def _gla_tables(chunk):
    levels = chunk.bit_length() - 1
    idx = np.arange(chunk)
    prefix = np.zeros((1 + 2 * levels, chunk, chunk), np.float32)
    prefix[0] = idx[None, :] <= idx[:, None]
    for lv in range(levels):
        m = 1 << lv
        mid = (idx // (2 * m)) * 2 * m + m
        for t in range(chunk):
            if t >= mid[t]:
                prefix[1 + lv, t, mid[t]:t + 1] = 1.0
            else:
                prefix[1 + levels + lv, t, t + 1:mid[t]] = 1.0
    x = idx[:, None] ^ idx[None, :]
    level = np.where(idx[:, None] == idx[None, :], levels, np.floor(np.log2(np.maximum(x, 1))).astype(np.int32))
    level = np.where(idx[None, :] > idx[:, None], -1, level).astype(np.int32)
    return prefix.reshape(-1, chunk), level, levels


def _gla_kernel(q_ref, k_ref, v_ref, g_ref, al_ref, wa_ref, ba_ref, ng_ref, pre_ref, lvl_ref, s0_ref,
                o_ref, sT_ref, s_scr, *, chunk, levels, n_chunks, n_steps):
    c = pl.program_id(2)

    @pl.when(c == 0)
    def _():
        s_scr[...] = s0_ref[...]

    prefix = pre_ref[...]
    level = lvl_ref[...]
    ng = ng_ref[...]
    x = jnp.dot(al_ref[...].astype(BF16), wa_ref[...], preferred_element_type=F32) + ba_ref[...]
    hi, lo = _split_bf16(_log_sigmoid(x) * (1.0 / GLA_TAU))
    width = GLA_HEADS_PER_STEP * GLA_DK

    for ci in range(n_chunks):
        rows = pl.ds(ci * chunk, chunk)
        r0 = ci * chunk
        parts = jnp.dot(prefix, jnp.concatenate([hi[r0:r0 + chunk], lo[r0:r0 + chunk]], axis=1),
                        preferred_element_type=F32)
        sums_all = parts[:, :width] + parts[:, width:]
        for j in range(GLA_HEADS_PER_STEP):
            kl = slice(j * GLA_DK, (j + 1) * GLA_DK)
            vl = slice(j * GLA_DV, (j + 1) * GLA_DV)
            sums = sums_all[:, kl]
            b = sums[:chunk]
            b_last = b[chunk - 1:chunk, :]
            decay = jnp.exp(sums[chunk:])
            q = q_ref[rows, kl] * (GLA_DK ** -0.5)
            k = k_ref[rows, kl]
            v = v_ref[rows, vl].astype(BF16)
            scores = jnp.where(level == levels, _dot_nt(q.astype(BF16), k.astype(BF16)), 0.0)
            for lv in range(levels):
                q_lv = (q * decay[lv * chunk:(lv + 1) * chunk]).astype(BF16)
                k_lv = (k * decay[(levels + lv) * chunk:(levels + lv + 1) * chunk]).astype(BF16)
                scores = jnp.where(level == lv, _dot_nt(q_lv, k_lv), scores)
            qd = (q * jnp.exp(b)).astype(BF16)
            s_t = s_scr[j]
            o = jnp.dot(scores.astype(BF16), v, preferred_element_type=F32) + _dot_nt(qd, s_t.astype(BF16))
            kdec = (k * jnp.exp(b_last - b)).astype(BF16)
            s_scr[j] = s_t * jnp.exp(b_last) + _dot_tn(v, kdec)
            ms = jnp.mean(o * o, axis=-1, keepdims=True)
            gg = g_ref[rows, vl]
            o_ref[rows, vl] = (o * lax.rsqrt(ms + NORM_EPS) * ng * (gg * jax.nn.sigmoid(gg))).astype(o_ref.dtype)

    @pl.when(c == n_steps - 1)
    def _():
        sT_ref[...] = s_scr[...]


def _gla(proj, row0, batch, seq, wa_pad, ba, norm_g, s0_t):
    chunk = min(GLA_CHUNK, seq)
    tb = _tile(seq, (256, 128, 64, 32, 16, 8))
    n_chunks = tb // chunk
    n_steps = seq // tb
    rb0 = row0 // tb
    hp = GLA_HEADS_PER_STEP
    prefix, level, levels = _gla_tables(chunk)

    def rows(b, c):
        return rb0 + b * n_steps + c

    fixed = lambda b, h, c: (0, 0)
    kern = functools.partial(_gla_kernel, chunk=chunk, levels=levels, n_chunks=n_chunks, n_steps=n_steps)
    return pl.pallas_call(
        kern,
        grid=(batch, GLA_HEADS // hp, n_steps),
        in_specs=[
            pl.BlockSpec((tb, hp * GLA_DK), lambda b, h, c: (rows(b, c), COL_GLA_Q // (hp * GLA_DK) + h)),
            pl.BlockSpec((tb, hp * GLA_DK), lambda b, h, c: (rows(b, c), COL_GLA_K // (hp * GLA_DK) + h)),
            pl.BlockSpec((tb, hp * GLA_DV), lambda b, h, c: (rows(b, c), COL_GLA_V // (hp * GLA_DV) + h)),
            pl.BlockSpec((tb, hp * GLA_DV), lambda b, h, c: (rows(b, c), COL_GLA_G // (hp * GLA_DV) + h)),
            pl.BlockSpec((tb, LANE), lambda b, h, c: (rows(b, c), COL_GLA_A // LANE)),
            pl.BlockSpec((LANE, hp * GLA_DK), lambda b, h, c: (0, h)),
            pl.BlockSpec((1, hp * GLA_DK), lambda b, h, c: (0, h)),
            pl.BlockSpec((1, GLA_DV), fixed),
            pl.BlockSpec(prefix.shape, fixed),
            pl.BlockSpec(level.shape, fixed),
            pl.BlockSpec((None, hp, GLA_DV, GLA_DK), lambda b, h, c: (b, h, 0, 0)),
        ],
        out_specs=[
            pl.BlockSpec((tb, hp * GLA_DV), lambda b, h, c: (b * n_steps + c, h)),
            pl.BlockSpec((None, hp, GLA_DV, GLA_DK), lambda b, h, c: (b, h, 0, 0)),
        ],
        out_shape=[
            jax.ShapeDtypeStruct((batch * seq, W_GLA), BF16),
            jax.ShapeDtypeStruct((batch, GLA_HEADS, GLA_DV, GLA_DK), F32),
        ],
        scratch_shapes=[pltpu.VMEM((hp, GLA_DV, GLA_DK), F32)],
        compiler_params=_params(("parallel", "parallel", "arbitrary"), VMEM_LIMIT),
        name="gla",
    )(proj, proj, proj, proj, proj, wa_pad, ba, norm_g, jnp.asarray(prefix, BF16), jnp.asarray(level), s0_t)


def _sb_suffix_and_total():
    r2 = lax.broadcasted_iota(jnp.int32, (SB_BLOCK, 2 * SB_BLOCK), 0)
    c2 = lax.broadcasted_iota(jnp.int32, (SB_BLOCK, 2 * SB_BLOCK), 1)
    return jnp.where((r2 > c2) | (c2 >= SB_BLOCK), 1.0, 0.0).astype(BF16)


def _sb_causal(n_rows):
    row = lax.broadcasted_iota(jnp.int32, (n_rows, SB_BLOCK), 0)
    col = lax.broadcasted_iota(jnp.int32, (n_rows, SB_BLOCK), 1)
    return col < row


def _head_lanes(j):
    return slice(j * SB_HEAD, (j + 1) * SB_HEAD)


def _sb_blocks(qs, key_values, runs, accs, suffix_and_total, causal):
    tq = qs[0].shape[0]
    log_betas, parts = [], []
    for q, (kblk, _) in zip(qs, key_values):
        z = _dot_nt(q, kblk) * SB_SCALE
        sp = jnp.maximum(z, 0.0) + jnp.log1p(jnp.exp(-jnp.abs(z)))
        log_1m = -sp
        if causal is not None:
            log_1m = jnp.where(causal, log_1m, 0.0)
        log_betas.append(z - sp)
        parts.extend(_split_bf16(log_1m))
    sums = jnp.dot(jnp.concatenate(parts, axis=0), suffix_and_total, preferred_element_type=F32)
    new_runs, new_accs = [], []
    for j, (_, vblk) in enumerate(key_values):
        cs = sums[2 * j * tq:(2 * j + 1) * tq] + sums[(2 * j + 1) * tq:(2 * j + 2) * tq]
        att = jnp.exp(log_betas[j] + cs[:, :SB_BLOCK] + runs[j])
        if causal is not None:
            att = jnp.where(causal, att, 0.0)
        new_accs.append(accs[j] + jnp.dot(att.astype(BF16), vblk, preferred_element_type=F32))
        new_runs.append(runs[j] + cs[:, SB_BLOCK:])
    return new_runs, new_accs


def _sb_sweep(qs, key_value_block, n_blocks, runs, accs, suffix_and_total):
    nh = len(qs)

    def run_max(runs):
        return functools.reduce(jnp.maximum, [jnp.max(r) for r in runs])

    def cond(carry):
        i, _, _, rmax = carry
        return jnp.logical_and(i < n_blocks, rmax > EXP_ZERO_BELOW)

    def body(carry):
        i, runs, accs, _ = carry
        rows = pl.ds(pl.multiple_of((n_blocks - 1 - i) * SB_BLOCK, SB_BLOCK), SB_BLOCK)
        runs, accs = _sb_blocks(qs, [key_value_block(rows, j) for j in range(nh)], runs, accs,
                                suffix_and_total, None)
        return i + 1, tuple(runs), tuple(accs), run_max(runs)

    _, _, accs, _ = lax.while_loop(cond, body, (jnp.int32(0), tuple(runs), tuple(accs), run_max(runs)))
    return accs


def _sb_finish(accs, g_ref, o_ref):
    for j, acc in enumerate(accs):
        ms = jnp.mean(acc * acc, axis=-1, keepdims=True)
        o_ref[:, _head_lanes(j)] = (acc * lax.rsqrt(ms + NORM_EPS) * g_ref[:, _head_lanes(j)]).astype(o_ref.dtype)


def _sb_prompt_kernel(q_ref, k_ref, v_ref, g_ref, o_ref):
    qi = pl.program_id(2)
    suffix_and_total = _sb_suffix_and_total()
    causal = _sb_causal(SB_BLOCK)
    zero = jnp.zeros((SB_BLOCK, SB_BLOCK), F32)

    def key_value_block(rows, j):
        return k_ref[rows, _head_lanes(j)].astype(BF16), v_ref[rows, _head_lanes(j)].astype(BF16)

    diag = pl.ds(pl.multiple_of(qi * SB_BLOCK, SB_BLOCK), SB_BLOCK)
    heads = range(SB_HEADS_PER_STEP)
    qs = [q_ref[:, _head_lanes(j)].astype(BF16) for j in heads]
    runs, accs = _sb_blocks(qs, [key_value_block(diag, j) for j in heads], [zero] * len(heads), [zero] * len(heads),
                            suffix_and_total, causal)
    _sb_finish(_sb_sweep(qs, key_value_block, qi, runs, accs, suffix_and_total), g_ref, o_ref)


def _sb_prompt(proj, batch, seq, norm_g):
    nq = seq // SB_BLOCK
    wide = SB_HEADS_PER_STEP * SB_HEAD
    return pl.pallas_call(
        _sb_prompt_kernel,
        grid=(batch, SB_HEADS // SB_HEADS_PER_STEP, nq),
        in_specs=[
            pl.BlockSpec((SB_BLOCK, wide), lambda b, h, i: (b * nq + i, COL_SB_Q // wide + h)),
            pl.BlockSpec((seq, wide), lambda b, h, i: (b, COL_SB_K // wide + h)),
            pl.BlockSpec((seq, wide), lambda b, h, i: (b, COL_SB_V // wide + h)),
            pl.BlockSpec((1, wide), lambda b, h, i: (0, h)),
        ],
        out_specs=pl.BlockSpec((SB_BLOCK, wide), lambda b, h, i: (b * nq + i, h)),
        out_shape=jax.ShapeDtypeStruct((batch * seq, W_SB), BF16),
        compiler_params=_params(("parallel", "parallel", "arbitrary"), VMEM_LIMIT),
        name="sb_prompt",
    )(proj, proj, proj, norm_g)


def _sb_sample_kernel(q_ref, k_ref, v_ref, kp_ref, vp_ref, g_ref, o_ref, *, seq, n_past_blocks):
    suffix_and_total = _sb_suffix_and_total()
    causal = _sb_causal(seq)
    pad = jnp.zeros((SB_BLOCK - seq, SB_HEAD), BF16)
    zero = jnp.zeros((seq, SB_BLOCK), F32)

    def past_block(rows, j):
        return kp_ref[rows, j, :].astype(BF16), vp_ref[rows, j, :].astype(BF16)

    def new_block(j):
        return (jnp.concatenate([k_ref[:, _head_lanes(j)].astype(BF16), pad], axis=0),
                jnp.concatenate([v_ref[:, _head_lanes(j)].astype(BF16), pad], axis=0))

    heads = range(SB_HEADS)
    qs = [q_ref[:, _head_lanes(j)].astype(BF16) for j in heads]
    runs, accs = _sb_blocks(qs, [new_block(j) for j in heads], [zero] * len(heads), [zero] * len(heads),
                            suffix_and_total, causal)
    _sb_finish(_sb_sweep(qs, past_block, n_past_blocks, runs, accs, suffix_and_total), g_ref, o_ref)


def _sb_sample(proj, row0, batch, seq, layer, cache_k, cache_v, norm_g):
    past = cache_k.shape[2]
    rb0 = row0 // seq
    kern = functools.partial(_sb_sample_kernel, seq=seq, n_past_blocks=past // SB_BLOCK)
    cache_spec = pl.BlockSpec((None, None, past, SB_HEADS, SB_HEAD), lambda b: (layer, b, 0, 0, 0))
    return pl.pallas_call(
        kern,
        grid=(batch,),
        in_specs=[
            pl.BlockSpec((seq, W_SB), lambda b: (rb0 + b, COL_SB_Q // W_SB)),
            pl.BlockSpec((seq, W_SB), lambda b: (rb0 + b, COL_SB_K // W_SB)),
            pl.BlockSpec((seq, W_SB), lambda b: (rb0 + b, COL_SB_V // W_SB)),
            cache_spec, cache_spec,
            pl.BlockSpec((1, W_SB), lambda b: (0, 0)),
        ],
        out_specs=pl.BlockSpec((seq, W_SB), lambda b: (b, 0)),
        out_shape=jax.ShapeDtypeStruct((batch * seq, W_SB), BF16),
        compiler_params=_params(("parallel",), VMEM_LIMIT),
        name="sb_sample",
    )(proj, proj, proj, cache_k, cache_v, norm_g)


def _rwkv_prep_kernel(p_ref, shift_ref, mu_ref, w0_ref, a0_ref, w2_ref, a2_ref, g2_ref,
                      r_out, w_out, k_out, v_out, a_out, g_out, last_out, prev_scr, *, tiles_per_seq):
    @pl.when(pl.program_id(0) % tiles_per_seq == 0)
    def _():
        prev_scr[...] = shift_ref[...]

    p = p_ref[...]
    first = lax.broadcasted_iota(jnp.int32, p.shape, 0) == 0
    prev = jnp.where(first, prev_scr[...], pltpu.roll(p, 1, 0))
    last = p[p.shape[0] - 1:, :]
    prev_scr[...] = last
    last_out[...] = last
    pm = p + (prev - p) * mu_ref[...]
    o = 0
    r = pm[:, o:o + W_RWKV]; o += W_RWKV
    w_low = pm[:, o:o + D_LORA]; o += D_LORA
    k = pm[:, o:o + W_RWKV]; o += W_RWKV
    v = pm[:, o:o + W_RWKV]; o += W_RWKV
    a_low = pm[:, o:o + D_LORA]; o += D_LORA
    g_low = pm[:, o:o + D_GATE_PAD]
    log_w = -math.exp(-0.5) * jax.nn.sigmoid(
        w0_ref[...] + jnp.dot(jnp.tanh(w_low).astype(BF16), w2_ref[...], preferred_element_type=F32))
    a = jax.nn.sigmoid(a0_ref[...] + jnp.dot(a_low.astype(BF16), a2_ref[...], preferred_element_type=F32))
    g = jnp.dot(jax.nn.sigmoid(g_low).astype(BF16), g2_ref[...], preferred_element_type=F32)
    r_out[...] = r
    w_out[...] = jnp.exp(log_w)
    k_out[...] = k
    v_out[...] = v
    a_out[...] = a
    g_out[...] = g


def _rwkv_prep(proj, row0, batch, seq, shift, mu, w0, a0, w2, a2, g2):
    tm = _tile(seq, (128, 64, 32, 16, 8))
    nt = seq // tm
    m = batch * seq
    rb0 = row0 // tm
    row = lambda i: (i, 0)
    fixed = lambda i: (0, 0)
    per_seq = pl.BlockSpec((None, 1, P_RWKV_PAD), lambda i: (i // nt, 0, 0))
    vec = pl.BlockSpec((1, W_RWKV), fixed)
    out = pl.BlockSpec((tm, W_RWKV), row)
    return pl.pallas_call(
        functools.partial(_rwkv_prep_kernel, tiles_per_seq=nt),
        grid=(m // tm,),
        in_specs=[pl.BlockSpec((tm, P_RWKV_PAD), lambda i: (rb0 + i, 0)), per_seq,
                  pl.BlockSpec((1, P_RWKV_PAD), fixed), vec, vec,
                  pl.BlockSpec((D_LORA, W_RWKV), fixed), pl.BlockSpec((D_LORA, W_RWKV), fixed),
                  pl.BlockSpec((D_GATE_PAD, W_RWKV), fixed)],
        out_specs=[out] * 6 + [per_seq],
        out_shape=[jax.ShapeDtypeStruct((m, W_RWKV), F32)] * 6
        + [jax.ShapeDtypeStruct((batch, 1, P_RWKV_PAD), F32)],
        scratch_shapes=[pltpu.VMEM((1, P_RWKV_PAD), F32)],
        compiler_params=_params(("arbitrary",), VMEM_LIMIT),
        name="rwkv_prep",
    )(proj, shift, mu, w0, a0, w2, a2, g2)


def _spread_lanes(copies):
    lanes = copies[0].shape[-1]
    full = lanes // LANE
    pieces = [c[..., i * LANE:(i + 1) * LANE] for i in range(full) for c in copies]
    if lanes % LANE:
        pieces += [c[..., full * LANE:] for c in copies]
    return pieces[0] if len(pieces) == 1 else jnp.concatenate(pieces, axis=-1)


def _gather_lanes(x, n_copies):
    lanes = x.shape[-1] // n_copies
    full, rem = lanes // LANE, lanes % LANE
    out = []
    for g in range(n_copies):
        pieces = [x[..., (i * n_copies + g) * LANE:(i * n_copies + g + 1) * LANE] for i in range(full)]
        if rem:
            start = full * n_copies * LANE + g * rem
            pieces.append(x[..., start:start + rem])
        out.append(pieces[0] if len(pieces) == 1 else jnp.concatenate(pieces, axis=-1))
    return out


def _rwkv_scan_kernel(r_ref, w_ref, k_ref, v_ref, a_ref, rk_ref, lng_ref, lnb_ref, kkp_ref, kap_ref, s0_ref,
                      z_ref, sT_ref, s_scr, kk_scr, b_scr, w_scr, k_scr, r_scr, v_scr, y_scr,
                      *, t_block, n_steps, vs):
    c = pl.program_id(1)
    n = RWKV_HEAD
    nv = n // vs
    lanes_in = r_ref.shape[-1]
    lanes_work = lanes_in * vs

    def dup(x):
        return _spread_lanes([x] * vs)

    @pl.when(c == 0)
    def _():
        s_scr[...] = s0_ref[...]

    def modified_key():
        return k_ref[...] * (1.0 + (a_ref[...] - 1.0) * kap_ref[...])

    kk_raw = k_ref[...] * kkp_ref[...]
    norm = jnp.sqrt(jnp.sum(kk_raw * kk_raw, axis=1, keepdims=True))
    kk = kk_raw / jnp.maximum(norm, 1e-12)
    kk_scr[...] = dup(kk)
    b_scr[...] = dup(kk * a_ref[...])
    w_scr[...] = dup(w_ref[...])
    k_scr[...] = dup(modified_key())
    r_scr[...] = dup(r_ref[...])
    v_scr[...] = _spread_lanes([v_ref[:, g * nv:(g + 1) * nv, :] for g in range(vs)])

    def step(t, carry):
        for lg in range(lanes_work // LANE):
            ls = pl.ds(lg * LANE, LANE)

            def value_rows(vo, carry2):
                v0 = pl.multiple_of(vo * SUBLANE, SUBLANE)
                v_tile = v_scr[t, pl.ds(v0, SUBLANE), ls]
                y_rows = []
                for vi in range(SUBLANE):
                    s_v = s_scr[v0 + vi, :, ls]
                    sa = jnp.sum(s_v * kk_scr[t, :, ls], axis=0, keepdims=True)
                    s_new = s_v * w_scr[t, :, ls] - sa * b_scr[t, :, ls] + v_tile[vi:vi + 1, :] * k_scr[t, :, ls]
                    s_scr[v0 + vi, :, ls] = s_new
                    y_rows.append(jnp.sum(s_new * r_scr[t, :, ls], axis=0, keepdims=True))
                y_scr[t, pl.ds(v0, SUBLANE), ls] = jnp.concatenate(y_rows, axis=0)
                return carry2

            lax.fori_loop(0, nv // SUBLANE, value_rows, 0)
        return carry

    lax.fori_loop(0, t_block, step, 0)

    y_groups = _gather_lanes(y_scr[...], vs)
    y = y_groups[0] if vs == 1 else jnp.concatenate(y_groups, axis=1)
    mean = jnp.mean(y, axis=1, keepdims=True)
    d = y - mean
    var = jnp.mean(d * d, axis=1, keepdims=True)
    bonus = jnp.sum(r_ref[...] * modified_key() * rk_ref[...], axis=1, keepdims=True) * v_ref[...]
    z_ref[...] = d * lax.rsqrt(var + RWKV_GN_EPS) * lng_ref[...] + lnb_ref[...] + bonus

    @pl.when(c == n_steps - 1)
    def _():
        sT_ref[...] = s_scr[...]


def _rwkv_scan(r, w, k, v, a, rk, lng, lnb, kkp, kap, s0, vs):
    seq, n, lanes = r.shape
    nv = n // vs
    t_block = _tile(seq, (8, 4, 2, 1))
    n_steps = seq // t_block
    lb = lanes if vs > 1 else _tile(lanes, (3 * LANE, 2 * LANE, LANE))
    lbw = lb * vs
    seq_spec = pl.BlockSpec((t_block, n, lb), lambda g, c: (c, 0, g))
    vec_spec = pl.BlockSpec((n, lb), lambda g, c: (0, g))
    st_spec = pl.BlockSpec((nv, n, lbw), lambda g, c: (0, 0, g))
    kern = functools.partial(_rwkv_scan_kernel, t_block=t_block, n_steps=n_steps, vs=vs)
    work = lambda rows: pltpu.VMEM((t_block, rows, lbw), F32)
    return pl.pallas_call(
        kern,
        grid=(lanes // lb, n_steps),
        in_specs=[seq_spec] * 5 + [vec_spec] * 5 + [st_spec],
        out_specs=[seq_spec, st_spec],
        out_shape=[jax.ShapeDtypeStruct((seq, n, lanes), F32), jax.ShapeDtypeStruct((nv, n, lanes * vs), F32)],
        scratch_shapes=[pltpu.VMEM((nv, n, lbw), F32), work(n), work(n), work(n), work(n), work(n),
                        work(nv), work(nv)],
        compiler_params=_params(("parallel", "arbitrary"), VMEM_LIMIT),
        name="rwkv_scan",
    )(r, w, k, v, a, rk, lng, lnb, kkp, kap, s0)


def _merge_kernel(gla_p, z_p, g_p, sb_p, gla_s, z_s, g_s, sb_s, o_ref, *, n_prompt_tiles):
    def write(gla_ref, z_ref, g_ref, sb_ref):
        o_ref[:, 0:W_GLA] = gla_ref[...]
        o_ref[:, W_GLA:W_GLA + W_RWKV] = (z_ref[...] * g_ref[...]).astype(o_ref.dtype)
        o_ref[:, W_GLA + W_RWKV:D_MODEL] = sb_ref[...]

    i = pl.program_id(0)

    @pl.when(i < n_prompt_tiles)
    def _():
        write(gla_p, z_p, g_p, sb_p)

    @pl.when(i >= n_prompt_tiles)
    def _():
        write(gla_s, z_s, g_s, sb_s)


def _merge(prompt, sample):
    mp, ms = prompt[1].shape[0], sample[1].shape[0]
    tm = _tile(math.gcd(mp, ms), (512, 256, 128, 64, 32, 16))
    n_p, n_s = mp // tm, ms // tm
    widths = (W_GLA, W_RWKV, W_RWKV, W_SB)
    p_specs = [pl.BlockSpec((tm, wd), lambda i: (jnp.minimum(i, n_p - 1), 0)) for wd in widths]
    s_specs = [pl.BlockSpec((tm, wd), lambda i: (jnp.maximum(i - n_p, 0), 0)) for wd in widths]
    return pl.pallas_call(
        functools.partial(_merge_kernel, n_prompt_tiles=n_p),
        grid=(n_p + n_s,),
        in_specs=p_specs + s_specs,
        out_specs=pl.BlockSpec((tm, D_MODEL), lambda i: (i, 0)),
        out_shape=jax.ShapeDtypeStruct((mp + ms, D_MODEL), BF16),
        compiler_params=_params(("parallel",), VMEM_LIMIT),
        name="merge",
    )(*prompt, *sample)


def _to_scan_layout(x, batch, seq):
    x = x.reshape(batch, seq, RWKV_HEADS, RWKV_HEAD).transpose(1, 3, 0, 2)
    return x.reshape(seq, RWKV_HEAD, batch * RWKV_HEADS)


def _from_scan_layout(z, batch, seq):
    z = z.reshape(seq, RWKV_HEAD, batch, RWKV_HEADS)
    return z.transpose(2, 0, 3, 1).reshape(batch * seq, W_RWKV)


def _head_vec_to_scan_layout(x, batch):
    x = x.reshape(RWKV_HEADS, RWKV_HEAD).T
    return jnp.tile(x[:, None, :], (1, batch, 1)).reshape(RWKV_HEAD, batch * RWKV_HEADS)


def _rwkv_group(ins, batch, seq, params, s0):
    lanes = batch * RWKV_HEADS
    n = RWKV_HEAD
    vs = next(f for f in (1, 2, 4, 8) if (lanes * f) % LANE == 0)
    nv = n // vs
    s0_l = s0.transpose(2, 3, 0, 1).reshape(n, n, lanes)
    s0_l = _spread_lanes([s0_l[g * nv:(g + 1) * nv] for g in range(vs)])
    vecs = [_head_vec_to_scan_layout(x, batch) for x in params]
    z, s_t = _rwkv_scan(*ins, *vecs, s0_l, vs)
    s_new = jnp.concatenate(_gather_lanes(s_t, vs), axis=0).reshape(n, n, batch, RWKV_HEADS).transpose(2, 3, 0, 1)
    return _from_scan_layout(z, batch, seq), s_new


W_IN_SEGMENTS = (
    (COL_RWKV, P_GLA, P_RWKV, P_RWKV_PAD - P_RWKV),
    (COL_GLA_Q, 0, GLA_QK, 0),
    (COL_SB_Q, P_GLA + P_RWKV, P_SB, 0),
    (COL_GLA_V, 2 * GLA_QK, W_GLA, 0),
    (COL_GLA_G, 2 * GLA_QK + W_GLA, W_GLA, 0),
    (COL_GLA_K, GLA_QK, GLA_QK, 0),
    (COL_GLA_A, 2 * GLA_QK + 2 * W_GLA, GLA_LOWRANK, P_PAD - COL_GLA_A - GLA_LOWRANK),
)


def _rearrange_w_in_kernel(w_ref, o_ref):
    rows = w_ref.shape[0]
    for dst, src, width, pad in W_IN_SEGMENTS:
        piece = w_ref[:, src:src + width].astype(BF16)
        if pad:
            piece = jnp.concatenate([piece, jnp.zeros((rows, pad), BF16)], axis=1)
        o_ref[:, dst:dst + width + pad] = piece


def _rearrange_w_in(w_in, layer):
    _, d, p = w_in.shape
    tk = _tile(d, (64, 32, 16))
    return pl.pallas_call(
        _rearrange_w_in_kernel,
        grid=(d // tk,),
        in_specs=[pl.BlockSpec((None, tk, p), lambda i: (layer, i, 0))],
        out_specs=pl.BlockSpec((tk, P_PAD), lambda i: (i, 0)),
        out_shape=jax.ShapeDtypeStruct((d, P_PAD), BF16),
        compiler_params=_params(("parallel",), VMEM_LIMIT),
        name="rearrange_w_in",
    )(w_in)


def _cast_pad_cols_kernel(w_ref, o_ref):
    rows, n = w_ref.shape
    pad = o_ref.shape[1] - n
    o_ref[:, :n] = w_ref[...].astype(BF16)
    if pad:
        o_ref[:, n:] = jnp.zeros((rows, pad), BF16)


def _cast_pad_cols(w, layer, n_out):
    _, k, n = w.shape
    tk = _tile(k, (128, 64, 32, 16))
    return pl.pallas_call(
        _cast_pad_cols_kernel,
        grid=(k // tk,),
        in_specs=[pl.BlockSpec((None, tk, n), lambda i: (layer, i, 0))],
        out_specs=pl.BlockSpec((tk, n_out), lambda i: (i, 0)),
        out_shape=jax.ShapeDtypeStruct((k, n_out), BF16),
        compiler_params=_params(("parallel",), VMEM_LIMIT),
        name="cast_pad_cols",
    )(w)


def _cast_pad_rows_kernel(w_ref, o_ref, *, n_data_tiles):
    @pl.when(pl.program_id(0) < n_data_tiles)
    def _():
        o_ref[...] = w_ref[...].astype(BF16)

    @pl.when(pl.program_id(0) >= n_data_tiles)
    def _():
        o_ref[...] = jnp.zeros(o_ref.shape, BF16)


def _cast_pad_rows(w, layer, k_out):
    _, k, n = w.shape
    tk = math.gcd(k, k_out - k)
    n_data = k // tk
    return pl.pallas_call(
        functools.partial(_cast_pad_rows_kernel, n_data_tiles=n_data),
        grid=(k_out // tk,),
        in_specs=[pl.BlockSpec((None, tk, n), lambda i: (layer, jnp.minimum(i, n_data - 1), 0))],
        out_specs=pl.BlockSpec((tk, n), lambda i: (i, 0)),
        out_shape=jax.ShapeDtypeStruct((k_out, n), BF16),
        compiler_params=_params(("parallel",), VMEM_LIMIT),
        name="cast_pad_rows",
    )(w)


def _heads_out_kernel(x_ref, o_ref):
    for h in range(SB_HEADS):
        o_ref[:, h, :] = x_ref[:, _head_lanes(h)]


def _heads_out(proj, col0, row0, rows):
    tm = _tile(rows, (512, 256, 128, 64, 32, 16, 8))
    rb0 = row0 // tm
    return pl.pallas_call(
        _heads_out_kernel,
        grid=(rows // tm,),
        in_specs=[pl.BlockSpec((tm, W_SB), lambda i: (rb0 + i, col0 // W_SB))],
        out_specs=pl.BlockSpec((tm, SB_HEADS, SB_HEAD), lambda i: (i, 0, 0)),
        out_shape=jax.ShapeDtypeStruct((rows, SB_HEADS, SB_HEAD), F32),
        compiler_params=_params(("parallel",), VMEM_LIMIT),
        name="heads_out",
    )(proj)


def kernel(x_prompt, x_sample, state_gla, state_rwkv, state_rwkv_shift, cache_sb_k, cache_sb_v, norm1_g, w_in, gla_wa2, gla_ba, gla_norm_g, rwkv_mu, rwkv_w0, rwkv_w2, rwkv_a0, rwkv_a2, rwkv_g2, rwkv_k_k, rwkv_k_a, rwkv_r_k, rwkv_ln_g, rwkv_ln_b, sb_norm_g, w_out, norm2_g, w_gate, w_up, w_down, final_g):
    bp, tp, d = x_prompt.shape
    bs, ts, _ = x_sample.shape
    depth = w_in.shape[0]
    mp, ms = bp * tp, bs * ts
    x_parts = (x_prompt.reshape(mp, d), x_sample.reshape(ms, d))
    x = None

    outs = {name: [] for name in ("gla_p", "gla_s", "rw_p", "rw_s", "sh_p", "sh_s", "k_p", "k_s", "v_p", "v_s")}
    row1 = lambda a: a.reshape(1, -1)
    for l in range(depth):
        h = _rmsnorm_joined(*x_parts, norm1_g[l], BF16) if x is None else _rmsnorm(x, norm1_g[l], BF16)
        proj = _matmul(h, _rearrange_w_in(w_in, l), F32)
        for nm, c0 in (("k", COL_SB_K), ("v", COL_SB_V)):
            outs[nm + "_p"].append(_heads_out(proj, c0, 0, mp).reshape(bp, tp, SB_HEADS, SB_HEAD))
            outs[nm + "_s"].append(_heads_out(proj, c0, mp, ms).reshape(bs, ts, SB_HEADS, SB_HEAD))

        lane_pad = (0, P_RWKV_PAD - P_RWKV)
        prep_args = (row1(jnp.pad(rwkv_mu[l], lane_pad)), row1(rwkv_w0[l]), row1(rwkv_a0[l]),
                     rwkv_w2[l].astype(BF16), rwkv_a2[l].astype(BF16),
                     jnp.pad(rwkv_g2[l], ((0, D_GATE_PAD - D_GATE_LORA), (0, 0))).astype(BF16))
        vec_args = (rwkv_r_k[l].reshape(-1), rwkv_ln_g[l], rwkv_ln_b[l], rwkv_k_k[l], rwkv_k_a[l])
        *seqs_p, gate_p, last_p = _rwkv_prep(proj, 0, bp, tp, jnp.zeros((bp, 1, P_RWKV_PAD), F32), *prep_args)
        *seqs_s, gate_s, last_s = _rwkv_prep(proj, mp, bs, ts, jnp.pad(state_rwkv_shift[l], ((0, 0), (0, 0), lane_pad)),
                                             *prep_args)
        outs["sh_p"].append(last_p[:, :, :P_RWKV])
        outs["sh_s"].append(last_s[:, :, :P_RWKV])
        scan_in_p = [_to_scan_layout(s, bp, tp) for s in seqs_p]
        scan_in_s = [_to_scan_layout(s, bs, ts) for s in seqs_s]

        wa_pad = jnp.pad(gla_wa2[l], ((0, LANE - GLA_LOWRANK), (0, 0))).astype(BF16)
        gla_args = (wa_pad, row1(gla_ba[l]), row1(gla_norm_g[l]))
        o_gla_p, sg_p = _gla(proj, 0, bp, tp, *gla_args, jnp.zeros((bp, GLA_HEADS, GLA_DV, GLA_DK), F32))
        o_gla_s, sg_s = _gla(proj, mp, bs, ts, *gla_args, state_gla[l].swapaxes(-1, -2))
        outs["gla_p"].append(sg_p.swapaxes(-1, -2))
        outs["gla_s"].append(sg_s.swapaxes(-1, -2))

        sbg = row1(sb_norm_g[l])
        o_sb_p = _sb_prompt(proj, bp, tp, sbg)
        o_sb_s = _sb_sample(proj, mp, bs, ts, l, cache_sb_k, cache_sb_v, sbg)

        z_p, sr_p = _rwkv_group(scan_in_p, bp, tp, vec_args, jnp.zeros((bp, RWKV_HEADS, RWKV_HEAD, RWKV_HEAD), F32))
        z_s, sr_s = _rwkv_group(scan_in_s, bs, ts, vec_args, state_rwkv[l])
        outs["rw_p"].append(sr_p)
        outs["rw_s"].append(sr_s)

        merged = _merge((o_gla_p, z_p, gate_p, o_sb_p), (o_gla_s, z_s, gate_s, o_sb_s))
        w_o = w_out[l].astype(BF16)
        x = _matmul_residual_joined(merged, w_o, *x_parts) if x is None else _matmul_residual(merged, w_o, x, tk=d)

        h2 = _rmsnorm(x, norm2_g[l], BF16)
        act = _swiglu(h2, _cast_pad_cols(w_gate, l, D_FF_PAD), _cast_pad_cols(w_up, l, D_FF_PAD))
        x = _matmul_residual(act, _cast_pad_rows(w_down, l, D_FF_PAD), x, tk=D_FF_PAD // 4)

    y_p, y_s = _rmsnorm_split(x, final_g, mp)
    sd = state_gla.dtype
    st = lambda name, dt=None: jnp.stack(outs[name]) if dt is None else jnp.stack(outs[name]).astype(dt)
    return (y_p.reshape(bp, tp, d), y_s.reshape(bs, ts, d),
            st("gla_p", sd), st("gla_s", sd), st("rw_p", sd), st("rw_s", sd),
            st("sh_p"), st("sh_s"), st("k_p"), st("k_s"), st("v_p"), st("v_s"))
```

```python
import functools
import math

import jax
import jax.numpy as jnp
import numpy as np
from jax import lax
from jax.experimental import pallas as pl
from jax.experimental.pallas import tpu as pltpu

F32 = jnp.float32
BF16 = jnp.bfloat16

D_MODEL = 4096
NORM_EPS = 1e-5
W_GLA = 1536
W_RWKV = 1536
W_SB = 1024
GLA_HEADS = 6
GLA_DV = 256
GLA_DK = 128
GLA_QK = 768
GLA_LOWRANK = 16
GLA_TAU = 16.0
GLA_CHUNK = 64
GLA_HEADS_PER_STEP = 3
RWKV_HEAD = 64
RWKV_HEADS = 24
D_LORA = 128
D_GATE_LORA = 480
D_GATE_PAD = 512
RWKV_GN_EPS = 64e-5
SB_HEAD = 128
SB_HEADS = 8
SB_SCALE = SB_HEAD ** -0.5
SB_BLOCK = 128
SB_HEADS_PER_STEP = 8
D_FF = 11008
D_FF_PAD = 11264
P_GLA = 4624
P_RWKV = 5344
P_RWKV_PAD = 5376
P_SB = 3072

COL_RWKV = 0
COL_GLA_Q = P_RWKV_PAD
COL_SB_Q = COL_GLA_Q + GLA_QK
COL_SB_K = COL_SB_Q + W_SB
COL_SB_V = COL_SB_K + W_SB
COL_GLA_V = COL_SB_V + W_SB
COL_GLA_G = COL_GLA_V + W_GLA
COL_GLA_K = COL_GLA_G + W_GLA
COL_GLA_A = COL_GLA_K + GLA_QK
P_PAD = 13312

LANE = 128
SUBLANE = 8
VMEM_LIMIT = 56 * 1024 * 1024

EXP_ZERO_BELOW = -104.0


def _tile(n, candidates):
    for c in candidates:
        if n % c == 0:
            return c
    return n


def _params(sem, vmem=None):
    return pltpu.CompilerParams(dimension_semantics=sem, vmem_limit_bytes=vmem)


def _rmsnorm_kernel(x_ref, g_ref, o_ref):
    x = x_ref[...]
    ms = jnp.mean(x * x, axis=-1, keepdims=True)
    o_ref[...] = (x * lax.rsqrt(ms + NORM_EPS) * g_ref[...]).astype(o_ref.dtype)


def _rmsnorm(x, g, out_dtype):
    m, d = x.shape
    tm = _tile(m, (512, 256, 128, 64, 32, 16, 8))
    return pl.pallas_call(
        _rmsnorm_kernel,
        grid=(m // tm,),
        in_specs=[pl.BlockSpec((tm, d), lambda i: (i, 0)), pl.BlockSpec((1, d), lambda i: (0, 0))],
        out_specs=pl.BlockSpec((tm, d), lambda i: (i, 0)),
        out_shape=jax.ShapeDtypeStruct((m, d), out_dtype),
        compiler_params=_params(("parallel",), VMEM_LIMIT),
        name="rmsnorm",
    )(x, g.reshape(1, d))


def _two_part_specs(tm, width, n_first, col=None):
    col = col or (lambda *idx: 0)
    return [pl.BlockSpec((tm, width), lambda *idx: (jnp.minimum(idx[0], n_first - 1), col(*idx))),
            pl.BlockSpec((tm, width), lambda *idx: (jnp.maximum(idx[0] - n_first, 0), col(*idx)))]


def _rmsnorm_joined_kernel(a_ref, b_ref, g_ref, o_ref, *, n_first):
    def run(x_ref):
        _rmsnorm_kernel(x_ref, g_ref, o_ref)

    @pl.when(pl.program_id(0) < n_first)
    def _():
        run(a_ref)

    @pl.when(pl.program_id(0) >= n_first)
    def _():
        run(b_ref)


def _rmsnorm_joined(a, b, g, out_dtype):
    (ma, d), mb = a.shape, b.shape[0]
    tm = _tile(math.gcd(ma, mb), (512, 256, 128, 64, 32, 16, 8))
    n_first = ma // tm
    return pl.pallas_call(
        functools.partial(_rmsnorm_joined_kernel, n_first=n_first),
        grid=((ma + mb) // tm,),
        in_specs=_two_part_specs(tm, d, n_first) + [pl.BlockSpec((1, d), lambda i: (0, 0))],
        out_specs=pl.BlockSpec((tm, d), lambda i: (i, 0)),
        out_shape=jax.ShapeDtypeStruct((ma + mb, d), out_dtype),
        compiler_params=_params(("parallel",), VMEM_LIMIT),
        name="rmsnorm_joined",
    )(a, b, g.reshape(1, d))


def _rmsnorm_split_kernel(x_ref, g_ref, a_ref, b_ref, *, n_first):
    x = x_ref[...]
    ms = jnp.mean(x * x, axis=-1, keepdims=True)
    y = x * lax.rsqrt(ms + NORM_EPS) * g_ref[...]
    i = pl.program_id(0)

    @pl.when(i < n_first)
    def _():
        a_ref[...] = y

    @pl.when(i >= n_first)
    def _():
        b_ref[...] = y


def _rmsnorm_split(x, g, m_first):
    m, d = x.shape
    tm = _tile(math.gcd(m_first, m - m_first), (512, 256, 128, 64, 32, 16, 8))
    n_first = m_first // tm
    return pl.pallas_call(
        functools.partial(_rmsnorm_split_kernel, n_first=n_first),
        grid=(m // tm,),
        in_specs=[pl.BlockSpec((tm, d), lambda i: (i, 0)), pl.BlockSpec((1, d), lambda i: (0, 0))],
        out_specs=[pl.BlockSpec((tm, d), lambda i: (jnp.minimum(i, n_first - 1), 0)),
                   pl.BlockSpec((tm, d), lambda i: (jnp.maximum(i - n_first, 0), 0))],
        out_shape=[jax.ShapeDtypeStruct((m_first, d), F32), jax.ShapeDtypeStruct((m - m_first, d), F32)],
        compiler_params=_params(("arbitrary",), VMEM_LIMIT),
        name="rmsnorm_split",
    )(x, g.reshape(1, d))


def _mm_kernel(x_ref, w_ref, o_ref):
    o_ref[...] = jnp.dot(x_ref[...], w_ref[...], preferred_element_type=F32).astype(o_ref.dtype)


def _mm_res_kernel(x_ref, w_ref, r_ref, o_ref):
    acc = jnp.dot(x_ref[...], w_ref[...], preferred_element_type=F32)
    k = pl.program_id(2)

    @pl.when(k == 0)
    def _():
        o_ref[...] = r_ref[...] + acc

    @pl.when(k != 0)
    def _():
        o_ref[...] += acc


def _swiglu_kernel(x_ref, wg_ref, wu_ref, o_ref):
    x = x_ref[...]
    g = jnp.dot(x, wg_ref[...], preferred_element_type=F32)
    u = jnp.dot(x, wu_ref[...], preferred_element_type=F32)
    o_ref[...] = (g * jax.nn.sigmoid(g) * u).astype(o_ref.dtype)


def _matmul(x, w, out_dtype):
    m, k = x.shape
    n = w.shape[1]
    tm = _tile(m, (1024, 512, 256, 128, 64, 32, 16, 8))
    tn = _tile(n, (1024, 512, 256, 128))
    return pl.pallas_call(
        _mm_kernel,
        grid=(m // tm, n // tn),
        in_specs=[pl.BlockSpec((tm, k), lambda i, j: (i, 0)), pl.BlockSpec((k, tn), lambda i, j: (0, j))],
        out_specs=pl.BlockSpec((tm, tn), lambda i, j: (i, j)),
        out_shape=jax.ShapeDtypeStruct((m, n), out_dtype),
        compiler_params=_params(("parallel", "parallel"), VMEM_LIMIT),
        name="matmul",
    )(x, w)


def _matmul_residual(x, w, res, tk):
    m, k = x.shape
    n = w.shape[1]
    tm = _tile(m, (1024, 512, 256, 128, 64, 32, 16, 8))
    tn = _tile(n, (1024, 512, 256, 128))
    return pl.pallas_call(
        _mm_res_kernel,
        grid=(m // tm, n // tn, k // tk),
        in_specs=[pl.BlockSpec((tm, tk), lambda i, j, kk: (i, kk)),
                  pl.BlockSpec((tk, tn), lambda i, j, kk: (kk, j)),
                  pl.BlockSpec((tm, tn), lambda i, j, kk: (i, j))],
        out_specs=pl.BlockSpec((tm, tn), lambda i, j, kk: (i, j)),
        out_shape=jax.ShapeDtypeStruct((m, n), F32),
        compiler_params=_params(("parallel", "parallel", "arbitrary"), VMEM_LIMIT),
        name="matmul_residual",
    )(x, w, res)


def _mm_res_full_kernel(x_ref, w_ref, r_ref, o_ref):
    o_ref[...] = r_ref[...] + jnp.dot(x_ref[...], w_ref[...], preferred_element_type=F32)


def _matmul_residual_resident(x, w, res):
    m, k = x.shape
    n = w.shape[1]
    tm = _tile(m, (1024, 512, 256, 128, 64, 32, 16, 8))
    tn = _tile(n, (256, 128))
    return pl.pallas_call(
        _mm_res_full_kernel,
        grid=(m // tm, n // tn),
        in_specs=[pl.BlockSpec((tm, k), lambda i, j: (i, 0), pipeline_mode=pl.Buffered(1)),
                  pl.BlockSpec((k, tn), lambda i, j: (0, j)),
                  pl.BlockSpec((tm, tn), lambda i, j: (i, j))],
        out_specs=pl.BlockSpec((tm, tn), lambda i, j: (i, j)),
        out_shape=jax.ShapeDtypeStruct((m, n), F32),
        compiler_params=_params(("parallel", "arbitrary"), VMEM_LIMIT),
        name="matmul_residual_resident",
    )(x, w, res)


def _mm_res_joined_kernel(x_ref, w_ref, ra_ref, rb_ref, o_ref, *, n_first):
    acc = jnp.dot(x_ref[...], w_ref[...], preferred_element_type=F32)

    @pl.when(pl.program_id(0) < n_first)
    def _():
        o_ref[...] = ra_ref[...] + acc

    @pl.when(pl.program_id(0) >= n_first)
    def _():
        o_ref[...] = rb_ref[...] + acc


def _matmul_residual_joined(x, w, res_a, res_b):
    m, k = x.shape
    n = w.shape[1]
    ma = res_a.shape[0]
    tm = _tile(math.gcd(ma, m - ma), (1024, 512, 256, 128, 64, 32, 16, 8))
    tn = _tile(n, (512, 256, 128))
    n_first = ma // tm
    return pl.pallas_call(
        functools.partial(_mm_res_joined_kernel, n_first=n_first),
        grid=(m // tm, n // tn),
        in_specs=[pl.BlockSpec((tm, k), lambda i, j: (i, 0)), pl.BlockSpec((k, tn), lambda i, j: (0, j))]
        + _two_part_specs(tm, tn, n_first, col=lambda i, j: j),
        out_specs=pl.BlockSpec((tm, tn), lambda i, j: (i, j)),
        out_shape=jax.ShapeDtypeStruct((m, n), F32),
        compiler_params=_params(("parallel", "parallel"), VMEM_LIMIT),
        name="matmul_residual_joined",
    )(x, w, res_a, res_b)


def _swiglu(x, wg, wu):
    m, k = x.shape
    n = wg.shape[1]
    tm = _tile(m, (1024, 512, 256, 128, 64, 32, 16, 8))
    tn = _tile(n, (512, 256, 128))
    return pl.pallas_call(
        _swiglu_kernel,
        grid=(m // tm, n // tn),
        in_specs=[pl.BlockSpec((tm, k), lambda i, j: (i, 0)),
                  pl.BlockSpec((k, tn), lambda i, j: (0, j)),
                  pl.BlockSpec((k, tn), lambda i, j: (0, j))],
        out_specs=pl.BlockSpec((tm, tn), lambda i, j: (i, j)),
        out_shape=jax.ShapeDtypeStruct((m, n), BF16),
        compiler_params=_params(("parallel", "parallel"), VMEM_LIMIT),
        name="swiglu",
    )(x, wg, wu)


def _log_sigmoid(x):
    return jnp.minimum(x, 0.0) - jnp.log1p(jnp.exp(-jnp.abs(x)))


def _split_bf16(x):
    hi = x.astype(BF16)
    lo = (x - hi.astype(F32)).astype(BF16)
    return hi, lo


def _dot_nt(a, b):
    return lax.dot_general(a, b, (((1,), (1,)), ((), ())), preferred_element_type=F32)


def _dot_tn(a, b):
    return lax.dot_general(a, b, (((0,), (0,)), ((), ())), preferred_element_type=F32)


def _gla_tables(chunk):
    levels = chunk.bit_length() - 1
    idx = np.arange(chunk)
    prefix = np.zeros((1 + 2 * levels, chunk, chunk), np.float32)
    prefix[0] = idx[None, :] <= idx[:, None]
    for lv in range(levels):
        m = 1 << lv
        mid = (idx // (2 * m)) * 2 * m + m
        for t in range(chunk):
            if t >= mid[t]:
                prefix[1 + lv, t, mid[t]:t + 1] = 1.0
            else:
                prefix[1 + levels + lv, t, t + 1:mid[t]] = 1.0
    x = idx[:, None] ^ idx[None, :]
    level = np.where(idx[:, None] == idx[None, :], levels, np.floor(np.log2(np.maximum(x, 1))).astype(np.int32))
    level = np.where(idx[None, :] > idx[:, None], -1, level).astype(np.int32)
    return prefix.reshape(-1, chunk), level, levels


def _gla_kernel(q_ref, k_ref, v_ref, g_ref, al_ref, wa_ref, ba_ref, ng_ref, pre_ref, lvl_ref, s0_ref,
                o_ref, sT_ref, s_scr, *, chunk, levels, n_chunks, n_steps):
    c = pl.program_id(2)

    @pl.when(c == 0)
    def _():
        s_scr[...] = s0_ref[...]

    prefix = pre_ref[...]
    level = lvl_ref[...]
    ng = ng_ref[...]
    x = jnp.dot(al_ref[...].astype(BF16), wa_ref[...], preferred_element_type=F32) + ba_ref[...]
    hi, lo = _split_bf16(_log_sigmoid(x) * (1.0 / GLA_TAU))
    width = GLA_HEADS_PER_STEP * GLA_DK

    for ci in range(n_chunks):
        rows = pl.ds(ci * chunk, chunk)
        r0 = ci * chunk
        parts = jnp.dot(prefix, jnp.concatenate([hi[r0:r0 + chunk], lo[r0:r0 + chunk]], axis=1),
                        preferred_element_type=F32)
        sums_all = parts[:, :width] + parts[:, width:]
        for j in range(GLA_HEADS_PER_STEP):
            kl = slice(j * GLA_DK, (j + 1) * GLA_DK)
            vl = slice(j * GLA_DV, (j + 1) * GLA_DV)
            sums = sums_all[:, kl]
            b = sums[:chunk]
            b_last = b[chunk - 1:chunk, :]
            decay = jnp.exp(sums[chunk:])
            q = q_ref[rows, kl] * (GLA_DK ** -0.5)
            k = k_ref[rows, kl]
            v = v_ref[rows, vl].astype(BF16)
            scores = jnp.where(level == levels, _dot_nt(q.astype(BF16), k.astype(BF16)), 0.0)
            for lv in range(levels):
                q_lv = (q * decay[lv * chunk:(lv + 1) * chunk]).astype(BF16)
                k_lv = (k * decay[(levels + lv) * chunk:(levels + lv + 1) * chunk]).astype(BF16)
                scores = jnp.where(level == lv, _dot_nt(q_lv, k_lv), scores)
            qd = (q * jnp.exp(b)).astype(BF16)
            s_t = s_scr[j]
            o = jnp.dot(scores.astype(BF16), v, preferred_element_type=F32) + _dot_nt(qd, s_t.astype(BF16))
            kdec = (k * jnp.exp(b_last - b)).astype(BF16)
            s_scr[j] = s_t * jnp.exp(b_last) + _dot_tn(v, kdec)
            ms = jnp.mean(o * o, axis=-1, keepdims=True)
            gg = g_ref[rows, vl]
            o_ref[rows, vl] = (o * lax.rsqrt(ms + NORM_EPS) * ng * (gg * jax.nn.sigmoid(gg))).astype(o_ref.dtype)

    @pl.when(c == n_steps - 1)
    def _():
        sT_ref[...] = s_scr[...]


def _gla(proj, row0, batch, seq, wa_pad, ba, norm_g, s0_t):
    chunk = min(GLA_CHUNK, seq)
    tb = _tile(seq, (256, 128, 64, 32, 16, 8))
    n_chunks = tb // chunk
    n_steps = seq // tb
    rb0 = row0 // tb
    hp = GLA_HEADS_PER_STEP
    prefix, level, levels = _gla_tables(chunk)

    def rows(b, c):
        return rb0 + b * n_steps + c

    fixed = lambda b, h, c: (0, 0)
    kern = functools.partial(_gla_kernel, chunk=chunk, levels=levels, n_chunks=n_chunks, n_steps=n_steps)
    return pl.pallas_call(
        kern,
        grid=(batch, GLA_HEADS // hp, n_steps),
        in_specs=[
            pl.BlockSpec((tb, hp * GLA_DK), lambda b, h, c: (rows(b, c), COL_GLA_Q // (hp * GLA_DK) + h)),
            pl.BlockSpec((tb, hp * GLA_DK), lambda b, h, c: (rows(b, c), COL_GLA_K // (hp * GLA_DK) + h)),
            pl.BlockSpec((tb, hp * GLA_DV), lambda b, h, c: (rows(b, c), COL_GLA_V // (hp * GLA_DV) + h)),
            pl.BlockSpec((tb, hp * GLA_DV), lambda b, h, c: (rows(b, c), COL_GLA_G // (hp * GLA_DV) + h)),
            pl.BlockSpec((tb, LANE), lambda b, h, c: (rows(b, c), COL_GLA_A // LANE)),
            pl.BlockSpec((LANE, hp * GLA_DK), lambda b, h, c: (0, h)),
            pl.BlockSpec((1, hp * GLA_DK), lambda b, h, c: (0, h)),
            pl.BlockSpec((1, GLA_DV), fixed),
            pl.BlockSpec(prefix.shape, fixed),
            pl.BlockSpec(level.shape, fixed),
            pl.BlockSpec((None, hp, GLA_DV, GLA_DK), lambda b, h, c: (b, h, 0, 0)),
        ],
        out_specs=[
            pl.BlockSpec((tb, hp * GLA_DV), lambda b, h, c: (b * n_steps + c, h)),
            pl.BlockSpec((None, hp, GLA_DV, GLA_DK), lambda b, h, c: (b, h, 0, 0)),
        ],
        out_shape=[
            jax.ShapeDtypeStruct((batch * seq, W_GLA), BF16),
            jax.ShapeDtypeStruct((batch, GLA_HEADS, GLA_DV, GLA_DK), F32),
        ],
        scratch_shapes=[pltpu.VMEM((hp, GLA_DV, GLA_DK), F32)],
        compiler_params=_params(("parallel", "parallel", "arbitrary"), VMEM_LIMIT),
        name="gla",
    )(proj, proj, proj, proj, proj, wa_pad, ba, norm_g, jnp.asarray(prefix, BF16), jnp.asarray(level), s0_t)


def _sb_suffix_and_total():
    r2 = lax.broadcasted_iota(jnp.int32, (SB_BLOCK, 2 * SB_BLOCK), 0)
    c2 = lax.broadcasted_iota(jnp.int32, (SB_BLOCK, 2 * SB_BLOCK), 1)
    return jnp.where((r2 > c2) | (c2 >= SB_BLOCK), 1.0, 0.0).astype(BF16)


def _sb_causal(n_rows):
    row = lax.broadcasted_iota(jnp.int32, (n_rows, SB_BLOCK), 0)
    col = lax.broadcasted_iota(jnp.int32, (n_rows, SB_BLOCK), 1)
    return col < row


def _head_lanes(j):
    return slice(j * SB_HEAD, (j + 1) * SB_HEAD)


def _sb_blocks(qs, key_values, runs, accs, suffix_and_total, causal):
    tq = qs[0].shape[0]
    log_betas, parts = [], []
    for q, (kblk, _) in zip(qs, key_values):
        z = _dot_nt(q, kblk) * SB_SCALE
        sp = jnp.maximum(z, 0.0) + jnp.log1p(jnp.exp(-jnp.abs(z)))
        log_1m = -sp
        if causal is not None:
            log_1m = jnp.where(causal, log_1m, 0.0)
        log_betas.append(z - sp)
        parts.extend(_split_bf16(log_1m))
    sums = jnp.dot(jnp.concatenate(parts, axis=0), suffix_and_total, preferred_element_type=F32)
    new_runs, new_accs = [], []
    for j, (_, vblk) in enumerate(key_values):
        cs = sums[2 * j * tq:(2 * j + 1) * tq] + sums[(2 * j + 1) * tq:(2 * j + 2) * tq]
        att = jnp.exp(log_betas[j] + cs[:, :SB_BLOCK] + runs[j])
        if causal is not None:
            att = jnp.where(causal, att, 0.0)
        new_accs.append(accs[j] + jnp.dot(att.astype(BF16), vblk, preferred_element_type=F32))
        new_runs.append(runs[j] + cs[:, SB_BLOCK:])
    return new_runs, new_accs


def _sb_sweep(qs, key_value_block, n_blocks, runs, accs, suffix_and_total):
    nh = len(qs)

    def run_max(runs):
        return functools.reduce(jnp.maximum, [jnp.max(r) for r in runs])

    def cond(carry):
        i, _, _, rmax = carry
        return jnp.logical_and(i < n_blocks, rmax > EXP_ZERO_BELOW)

    def body(carry):
        i, runs, accs, _ = carry
        rows = pl.ds(pl.multiple_of((n_blocks - 1 - i) * SB_BLOCK, SB_BLOCK), SB_BLOCK)
        runs, accs = _sb_blocks(qs, [key_value_block(rows, j) for j in range(nh)], runs, accs,
                                suffix_and_total, None)
        return i + 1, tuple(runs), tuple(accs), run_max(runs)

    _, _, accs, _ = lax.while_loop(cond, body, (jnp.int32(0), tuple(runs), tuple(accs), run_max(runs)))
    return accs


def _sb_finish(accs, g_ref, o_ref):
    for j, acc in enumerate(accs):
        ms = jnp.mean(acc * acc, axis=-1, keepdims=True)
        o_ref[:, _head_lanes(j)] = (acc * lax.rsqrt(ms + NORM_EPS) * g_ref[:, _head_lanes(j)]).astype(o_ref.dtype)


def _sb_prompt_kernel(q_ref, k_ref, v_ref, g_ref, o_ref):
    qi = pl.program_id(2)
    suffix_and_total = _sb_suffix_and_total()
    causal = _sb_causal(SB_BLOCK)
    zero = jnp.zeros((SB_BLOCK, SB_BLOCK), F32)

    def key_value_block(rows, j):
        return k_ref[rows, _head_lanes(j)].astype(BF16), v_ref[rows, _head_lanes(j)].astype(BF16)

    diag = pl.ds(pl.multiple_of(qi * SB_BLOCK, SB_BLOCK), SB_BLOCK)
    heads = range(SB_HEADS_PER_STEP)
    qs = [q_ref[:, _head_lanes(j)].astype(BF16) for j in heads]
    runs, accs = _sb_blocks(qs, [key_value_block(diag, j) for j in heads], [zero] * len(heads), [zero] * len(heads),
                            suffix_and_total, causal)
    _sb_finish(_sb_sweep(qs, key_value_block, qi, runs, accs, suffix_and_total), g_ref, o_ref)


def _sb_prompt(proj, batch, seq, norm_g):
    nq = seq // SB_BLOCK
    wide = SB_HEADS_PER_STEP * SB_HEAD
    return pl.pallas_call(
        _sb_prompt_kernel,
        grid=(batch, SB_HEADS // SB_HEADS_PER_STEP, nq),
        in_specs=[
            pl.BlockSpec((SB_BLOCK, wide), lambda b, h, i: (b * nq + i, COL_SB_Q // wide + h)),
            pl.BlockSpec((seq, wide), lambda b, h, i: (b, COL_SB_K // wide + h)),
            pl.BlockSpec((seq, wide), lambda b, h, i: (b, COL_SB_V // wide + h)),
            pl.BlockSpec((1, wide), lambda b, h, i: (0, h)),
        ],
        out_specs=pl.BlockSpec((SB_BLOCK, wide), lambda b, h, i: (b * nq + i, h)),
        out_shape=jax.ShapeDtypeStruct((batch * seq, W_SB), BF16),
        compiler_params=_params(("parallel", "parallel", "arbitrary"), VMEM_LIMIT),
        name="sb_prompt",
    )(proj, proj, proj, norm_g)


def _sb_sample_kernel(q_ref, k_ref, v_ref, kp_ref, vp_ref, g_ref, o_ref, *, seq, n_past_blocks):
    suffix_and_total = _sb_suffix_and_total()
    causal = _sb_causal(seq)
    pad = jnp.zeros((SB_BLOCK - seq, SB_HEAD), BF16)
    zero = jnp.zeros((seq, SB_BLOCK), F32)

    def past_block(rows, j):
        return kp_ref[rows, j, :].astype(BF16), vp_ref[rows, j, :].astype(BF16)

    def new_block(j):
        return (jnp.concatenate([k_ref[:, _head_lanes(j)].astype(BF16), pad], axis=0),
                jnp.concatenate([v_ref[:, _head_lanes(j)].astype(BF16), pad], axis=0))

    heads = range(SB_HEADS)
    qs = [q_ref[:, _head_lanes(j)].astype(BF16) for j in heads]
    runs, accs = _sb_blocks(qs, [new_block(j) for j in heads], [zero] * len(heads), [zero] * len(heads),
                            suffix_and_total, causal)
    _sb_finish(_sb_sweep(qs, past_block, n_past_blocks, runs, accs, suffix_and_total), g_ref, o_ref)


def _sb_sample(proj, row0, batch, seq, layer, cache_k, cache_v, norm_g):
    past = cache_k.shape[2]
    rb0 = row0 // seq
    kern = functools.partial(_sb_sample_kernel, seq=seq, n_past_blocks=past // SB_BLOCK)
    cache_spec = pl.BlockSpec((None, None, past, SB_HEADS, SB_HEAD), lambda b: (layer, b, 0, 0, 0))
    return pl.pallas_call(
        kern,
        grid=(batch,),
        in_specs=[
            pl.BlockSpec((seq, W_SB), lambda b: (rb0 + b, COL_SB_Q // W_SB)),
            pl.BlockSpec((seq, W_SB), lambda b: (rb0 + b, COL_SB_K // W_SB)),
            pl.BlockSpec((seq, W_SB), lambda b: (rb0 + b, COL_SB_V // W_SB)),
            cache_spec, cache_spec,
            pl.BlockSpec((1, W_SB), lambda b: (0, 0)),
        ],
        out_specs=pl.BlockSpec((seq, W_SB), lambda b: (b, 0)),
        out_shape=jax.ShapeDtypeStruct((batch * seq, W_SB), BF16),
        compiler_params=_params(("parallel",), VMEM_LIMIT),
        name="sb_sample",
    )(proj, proj, proj, cache_k, cache_v, norm_g)


def _rwkv_prep_kernel(p_ref, shift_ref, mu_ref, w0_ref, a0_ref, w2_ref, a2_ref, g2_ref,
                      r_out, w_out, k_out, v_out, a_out, g_out, last_out, prev_scr, *, tiles_per_seq):
    @pl.when(pl.program_id(0) % tiles_per_seq == 0)
    def _():
        prev_scr[...] = shift_ref[...]

    p = p_ref[...]
    first = lax.broadcasted_iota(jnp.int32, p.shape, 0) == 0
    prev = jnp.where(first, prev_scr[...], pltpu.roll(p, 1, 0))
    last = p[p.shape[0] - 1:, :]
    prev_scr[...] = last
    last_out[...] = last
    pm = p + (prev - p) * mu_ref[...]
    o = 0
    r = pm[:, o:o + W_RWKV]; o += W_RWKV
    w_low = pm[:, o:o + D_LORA]; o += D_LORA
    k = pm[:, o:o + W_RWKV]; o += W_RWKV
    v = pm[:, o:o + W_RWKV]; o += W_RWKV
    a_low = pm[:, o:o + D_LORA]; o += D_LORA
    g_low = pm[:, o:o + D_GATE_PAD]
    log_w = -math.exp(-0.5) * jax.nn.sigmoid(
        w0_ref[...] + jnp.dot(jnp.tanh(w_low).astype(BF16), w2_ref[...], preferred_element_type=F32))
    a = jax.nn.sigmoid(a0_ref[...] + jnp.dot(a_low.astype(BF16), a2_ref[...], preferred_element_type=F32))
    g = jnp.dot(jax.nn.sigmoid(g_low).astype(BF16), g2_ref[...], preferred_element_type=F32)
    r_out[...] = r
    w_out[...] = jnp.exp(log_w)
    k_out[...] = k
    v_out[...] = v
    a_out[...] = a
    g_out[...] = g


def _rwkv_prep(proj, row0, batch, seq, shift, mu, w0, a0, w2, a2, g2):
    tm = _tile(seq, (128, 64, 32, 16, 8))
    nt = seq // tm
    m = batch * seq
    rb0 = row0 // tm
    row = lambda i: (i, 0)
    fixed = lambda i: (0, 0)
    per_seq = pl.BlockSpec((None, 1, P_RWKV_PAD), lambda i: (i // nt, 0, 0))
    vec = pl.BlockSpec((1, W_RWKV), fixed)
    out = pl.BlockSpec((tm, W_RWKV), row)
    return pl.pallas_call(
        functools.partial(_rwkv_prep_kernel, tiles_per_seq=nt),
        grid=(m // tm,),
        in_specs=[pl.BlockSpec((tm, P_RWKV_PAD), lambda i: (rb0 + i, 0)), per_seq,
                  pl.BlockSpec((1, P_RWKV_PAD), fixed), vec, vec,
                  pl.BlockSpec((D_LORA, W_RWKV), fixed), pl.BlockSpec((D_LORA, W_RWKV), fixed),
                  pl.BlockSpec((D_GATE_PAD, W_RWKV), fixed)],
        out_specs=[out] * 6 + [per_seq],
        out_shape=[jax.ShapeDtypeStruct((m, W_RWKV), F32)] * 6
        + [jax.ShapeDtypeStruct((batch, 1, P_RWKV_PAD), F32)],
        scratch_shapes=[pltpu.VMEM((1, P_RWKV_PAD), F32)],
        compiler_params=_params(("arbitrary",), VMEM_LIMIT),
        name="rwkv_prep",
    )(proj, shift, mu, w0, a0, w2, a2, g2)


def _spread_lanes(copies):
    lanes = copies[0].shape[-1]
    full = lanes // LANE
    pieces = [c[..., i * LANE:(i + 1) * LANE] for i in range(full) for c in copies]
    if lanes % LANE:
        pieces += [c[..., full * LANE:] for c in copies]
    return pieces[0] if len(pieces) == 1 else jnp.concatenate(pieces, axis=-1)


def _gather_lanes(x, n_copies):
    lanes = x.shape[-1] // n_copies
    full, rem = lanes // LANE, lanes % LANE
    out = []
    for g in range(n_copies):
        pieces = [x[..., (i * n_copies + g) * LANE:(i * n_copies + g + 1) * LANE] for i in range(full)]
        if rem:
            start = full * n_copies * LANE + g * rem
            pieces.append(x[..., start:start + rem])
        out.append(pieces[0] if len(pieces) == 1 else jnp.concatenate(pieces, axis=-1))
    return out


def _rwkv_scan_kernel(r_ref, w_ref, k_ref, v_ref, a_ref, rk_ref, lng_ref, lnb_ref, kkp_ref, kap_ref, s0_ref,
                      z_ref, sT_ref, s_scr, kk_scr, b_scr, w_scr, k_scr, r_scr, v_scr, y_scr,
                      *, t_block, n_steps, vs):
    c = pl.program_id(1)
    n = RWKV_HEAD
    nv = n // vs
    lanes_in = r_ref.shape[-1]
    lanes_work = lanes_in * vs

    def dup(x):
        return _spread_lanes([x] * vs)

    @pl.when(c == 0)
    def _():
        s_scr[...] = s0_ref[...]

    def modified_key():
        return k_ref[...] * (1.0 + (a_ref[...] - 1.0) * kap_ref[...])

    kk_raw = k_ref[...] * kkp_ref[...]
    norm = jnp.sqrt(jnp.sum(kk_raw * kk_raw, axis=1, keepdims=True))
    kk = kk_raw / jnp.maximum(norm, 1e-12)
    kk_scr[...] = dup(kk)
    b_scr[...] = dup(kk * a_ref[...])
    w_scr[...] = dup(w_ref[...])
    k_scr[...] = dup(modified_key())
    r_scr[...] = dup(r_ref[...])
    v_scr[...] = _spread_lanes([v_ref[:, g * nv:(g + 1) * nv, :] for g in range(vs)])

    def step(t, carry):
        for lg in range(lanes_work // LANE):
            ls = pl.ds(lg * LANE, LANE)

            def value_rows(vo, carry2):
                v0 = pl.multiple_of(vo * SUBLANE, SUBLANE)
                v_tile = v_scr[t, pl.ds(v0, SUBLANE), ls]
                y_rows = []
                for vi in range(SUBLANE):
                    s_v = s_scr[v0 + vi, :, ls]
                    sa = jnp.sum(s_v * kk_scr[t, :, ls], axis=0, keepdims=True)
                    s_new = s_v * w_scr[t, :, ls] - sa * b_scr[t, :, ls] + v_tile[vi:vi + 1, :] * k_scr[t, :, ls]
                    s_scr[v0 + vi, :, ls] = s_new
                    y_rows.append(jnp.sum(s_new * r_scr[t, :, ls], axis=0, keepdims=True))
                y_scr[t, pl.ds(v0, SUBLANE), ls] = jnp.concatenate(y_rows, axis=0)
                return carry2

            lax.fori_loop(0, nv // SUBLANE, value_rows, 0)
        return carry

    lax.fori_loop(0, t_block, step, 0)

    y_groups = _gather_lanes(y_scr[...], vs)
    y = y_groups[0] if vs == 1 else jnp.concatenate(y_groups, axis=1)
    mean = jnp.mean(y, axis=1, keepdims=True)
    d = y - mean
    var = jnp.mean(d * d, axis=1, keepdims=True)
    bonus = jnp.sum(r_ref[...] * modified_key() * rk_ref[...], axis=1, keepdims=True) * v_ref[...]
    z_ref[...] = d * lax.rsqrt(var + RWKV_GN_EPS) * lng_ref[...] + lnb_ref[...] + bonus

    @pl.when(c == n_steps - 1)
    def _():
        sT_ref[...] = s_scr[...]


def _rwkv_scan(r, w, k, v, a, rk, lng, lnb, kkp, kap, s0, vs):
    seq, n, lanes = r.shape
    nv = n // vs
    t_block = _tile(seq, (8, 4, 2, 1))
    n_steps = seq // t_block
    lb = lanes if vs > 1 else _tile(lanes, (3 * LANE, 2 * LANE, LANE))
    lbw = lb * vs
    seq_spec = pl.BlockSpec((t_block, n, lb), lambda g, c: (c, 0, g))
    vec_spec = pl.BlockSpec((n, lb), lambda g, c: (0, g))
    st_spec = pl.BlockSpec((nv, n, lbw), lambda g, c: (0, 0, g))
    kern = functools.partial(_rwkv_scan_kernel, t_block=t_block, n_steps=n_steps, vs=vs)
    work = lambda rows: pltpu.VMEM((t_block, rows, lbw), F32)
    return pl.pallas_call(
        kern,
        grid=(lanes // lb, n_steps),
        in_specs=[seq_spec] * 5 + [vec_spec] * 5 + [st_spec],
        out_specs=[seq_spec, st_spec],
        out_shape=[jax.ShapeDtypeStruct((seq, n, lanes), F32), jax.ShapeDtypeStruct((nv, n, lanes * vs), F32)],
        scratch_shapes=[pltpu.VMEM((nv, n, lbw), F32), work(n), work(n), work(n), work(n), work(n),
                        work(nv), work(nv)],
        compiler_params=_params(("parallel", "arbitrary"), VMEM_LIMIT),
        name="rwkv_scan",
    )(r, w, k, v, a, rk, lng, lnb, kkp, kap, s0)


def _merge_kernel(gla_p, z_p, g_p, sb_p, gla_s, z_s, g_s, sb_s, o_ref, *, n_prompt_tiles):
    def write(gla_ref, z_ref, g_ref, sb_ref):
        o_ref[:, 0:W_GLA] = gla_ref[...]
        o_ref[:, W_GLA:W_GLA + W_RWKV] = (z_ref[...] * g_ref[...]).astype(o_ref.dtype)
        o_ref[:, W_GLA + W_RWKV:D_MODEL] = sb_ref[...]

    i = pl.program_id(0)

    @pl.when(i < n_prompt_tiles)
    def _():
        write(gla_p, z_p, g_p, sb_p)

    @pl.when(i >= n_prompt_tiles)
    def _():
        write(gla_s, z_s, g_s, sb_s)


def _merge(prompt, sample):
    mp, ms = prompt[1].shape[0], sample[1].shape[0]
    tm = _tile(math.gcd(mp, ms), (512, 256, 128, 64, 32, 16))
    n_p, n_s = mp // tm, ms // tm
    widths = (W_GLA, W_RWKV, W_RWKV, W_SB)
    p_specs = [pl.BlockSpec((tm, wd), lambda i: (jnp.minimum(i, n_p - 1), 0)) for wd in widths]
    s_specs = [pl.BlockSpec((tm, wd), lambda i: (jnp.maximum(i - n_p, 0), 0)) for wd in widths]
    return pl.pallas_call(
        functools.partial(_merge_kernel, n_prompt_tiles=n_p),
        grid=(n_p + n_s,),
        in_specs=p_specs + s_specs,
        out_specs=pl.BlockSpec((tm, D_MODEL), lambda i: (i, 0)),
        out_shape=jax.ShapeDtypeStruct((mp + ms, D_MODEL), BF16),
        compiler_params=_params(("parallel",), VMEM_LIMIT),
        name="merge",
    )(*prompt, *sample)


def _to_scan_layout(x, batch, seq):
    x = x.reshape(batch, seq, RWKV_HEADS, RWKV_HEAD).transpose(1, 3, 0, 2)
    return x.reshape(seq, RWKV_HEAD, batch * RWKV_HEADS)


def _from_scan_layout(z, batch, seq):
    z = z.reshape(seq, RWKV_HEAD, batch, RWKV_HEADS)
    return z.transpose(2, 0, 3, 1).reshape(batch * seq, W_RWKV)


def _head_vec_to_scan_layout(x, batch):
    x = x.reshape(RWKV_HEADS, RWKV_HEAD).T
    return jnp.tile(x[:, None, :], (1, batch, 1)).reshape(RWKV_HEAD, batch * RWKV_HEADS)


def _rwkv_group(ins, batch, seq, params, s0):
    lanes = batch * RWKV_HEADS
    n = RWKV_HEAD
    vs = next(f for f in (1, 2, 4, 8) if (lanes * f) % LANE == 0)
    nv = n // vs
    s0_l = s0.transpose(2, 3, 0, 1).reshape(n, n, lanes)
    s0_l = _spread_lanes([s0_l[g * nv:(g + 1) * nv] for g in range(vs)])
    vecs = [_head_vec_to_scan_layout(x, batch) for x in params]
    z, s_t = _rwkv_scan(*ins, *vecs, s0_l, vs)
    s_new = jnp.concatenate(_gather_lanes(s_t, vs), axis=0).reshape(n, n, batch, RWKV_HEADS).transpose(2, 3, 0, 1)
    return _from_scan_layout(z, batch, seq), s_new


W_IN_SEGMENTS = (
    (COL_RWKV, P_GLA, P_RWKV, P_RWKV_PAD - P_RWKV),
    (COL_GLA_Q, 0, GLA_QK, 0),
    (COL_SB_Q, P_GLA + P_RWKV, P_SB, 0),
    (COL_GLA_V, 2 * GLA_QK, W_GLA, 0),
    (COL_GLA_G, 2 * GLA_QK + W_GLA, W_GLA, 0),
    (COL_GLA_K, GLA_QK, GLA_QK, 0),
    (COL_GLA_A, 2 * GLA_QK + 2 * W_GLA, GLA_LOWRANK, P_PAD - COL_GLA_A - GLA_LOWRANK),
)


def _rearrange_w_in_kernel(w_ref, o_ref):
    rows = w_ref.shape[0]
    for dst, src, width, pad in W_IN_SEGMENTS:
        piece = w_ref[:, src:src + width].astype(BF16)
        if pad:
            piece = jnp.concatenate([piece, jnp.zeros((rows, pad), BF16)], axis=1)
        o_ref[:, dst:dst + width + pad] = piece


def _rearrange_w_in(w_in, layer):
    _, d, p = w_in.shape
    tk = _tile(d, (64, 32, 16))
    return pl.pallas_call(
        _rearrange_w_in_kernel,
        grid=(d // tk,),
        in_specs=[pl.BlockSpec((None, tk, p), lambda i: (layer, i, 0))],
        out_specs=pl.BlockSpec((tk, P_PAD), lambda i: (i, 0)),
        out_shape=jax.ShapeDtypeStruct((d, P_PAD), BF16),
        compiler_params=_params(("parallel",), VMEM_LIMIT),
        name="rearrange_w_in",
    )(w_in)


def _cast_pad_cols_kernel(w_ref, o_ref):
    rows, n = w_ref.shape
    pad = o_ref.shape[1] - n
    o_ref[:, :n] = w_ref[...].astype(BF16)
    if pad:
        o_ref[:, n:] = jnp.zeros((rows, pad), BF16)


def _cast_pad_cols(w, layer, n_out):
    _, k, n = w.shape
    tk = _tile(k, (128, 64, 32, 16))
    return pl.pallas_call(
        _cast_pad_cols_kernel,
        grid=(k // tk,),
        in_specs=[pl.BlockSpec((None, tk, n), lambda i: (layer, i, 0))],
        out_specs=pl.BlockSpec((tk, n_out), lambda i: (i, 0)),
        out_shape=jax.ShapeDtypeStruct((k, n_out), BF16),
        compiler_params=_params(("parallel",), VMEM_LIMIT),
        name="cast_pad_cols",
    )(w)


def _cast_pad_rows_kernel(w_ref, o_ref, *, n_data_tiles):
    @pl.when(pl.program_id(0) < n_data_tiles)
    def _():
        o_ref[...] = w_ref[...].astype(BF16)

    @pl.when(pl.program_id(0) >= n_data_tiles)
    def _():
        o_ref[...] = jnp.zeros(o_ref.shape, BF16)


def _cast_pad_rows(w, layer, k_out):
    _, k, n = w.shape
    tk = math.gcd(k, k_out - k)
    n_data = k // tk
    return pl.pallas_call(
        functools.partial(_cast_pad_rows_kernel, n_data_tiles=n_data),
        grid=(k_out // tk,),
        in_specs=[pl.BlockSpec((None, tk, n), lambda i: (layer, jnp.minimum(i, n_data - 1), 0))],
        out_specs=pl.BlockSpec((tk, n), lambda i: (i, 0)),
        out_shape=jax.ShapeDtypeStruct((k_out, n), BF16),
        compiler_params=_params(("parallel",), VMEM_LIMIT),
        name="cast_pad_rows",
    )(w)


def _heads_out_kernel(x_ref, o_ref):
    for h in range(SB_HEADS):
        o_ref[:, h, :] = x_ref[:, _head_lanes(h)]


def _heads_out(proj, col0, row0, rows):
    tm = _tile(rows, (512, 256, 128, 64, 32, 16, 8))
    rb0 = row0 // tm
    return pl.pallas_call(
        _heads_out_kernel,
        grid=(rows // tm,),
        in_specs=[pl.BlockSpec((tm, W_SB), lambda i: (rb0 + i, col0 // W_SB))],
        out_specs=pl.BlockSpec((tm, SB_HEADS, SB_HEAD), lambda i: (i, 0, 0)),
        out_shape=jax.ShapeDtypeStruct((rows, SB_HEADS, SB_HEAD), F32),
        compiler_params=_params(("parallel",), VMEM_LIMIT),
        name="heads_out",
    )(proj)


def kernel(x_prompt, x_sample, state_gla, state_rwkv, state_rwkv_shift, cache_sb_k, cache_sb_v, norm1_g, w_in, gla_wa2, gla_ba, gla_norm_g, rwkv_mu, rwkv_w0, rwkv_w2, rwkv_a0, rwkv_a2, rwkv_g2, rwkv_k_k, rwkv_k_a, rwkv_r_k, rwkv_ln_g, rwkv_ln_b, sb_norm_g, w_out, norm2_g, w_gate, w_up, w_down, final_g):
    bp, tp, d = x_prompt.shape
    bs, ts, _ = x_sample.shape
    depth = w_in.shape[0]
    mp, ms = bp * tp, bs * ts
    x_parts = (x_prompt.reshape(mp, d), x_sample.reshape(ms, d))
    x = None

    outs = {name: [] for name in ("gla_p", "gla_s", "rw_p", "rw_s", "sh_p", "sh_s", "k_p", "k_s", "v_p", "v_s")}
    row1 = lambda a: a.reshape(1, -1)
    for l in range(depth):
        h = _rmsnorm_joined(*x_parts, norm1_g[l], BF16) if x is None else _rmsnorm(x, norm1_g[l], BF16)
        proj = _matmul(h, _rearrange_w_in(w_in, l), F32)
        for nm, c0 in (("k", COL_SB_K), ("v", COL_SB_V)):
            outs[nm + "_p"].append(_heads_out(proj, c0, 0, mp).reshape(bp, tp, SB_HEADS, SB_HEAD))
            outs[nm + "_s"].append(_heads_out(proj, c0, mp, ms).reshape(bs, ts, SB_HEADS, SB_HEAD))

        lane_pad = (0, P_RWKV_PAD - P_RWKV)
        prep_args = (row1(jnp.pad(rwkv_mu[l], lane_pad)), row1(rwkv_w0[l]), row1(rwkv_a0[l]),
                     rwkv_w2[l].astype(BF16), rwkv_a2[l].astype(BF16),
                     jnp.pad(rwkv_g2[l], ((0, D_GATE_PAD - D_GATE_LORA), (0, 0))).astype(BF16))
        vec_args = (rwkv_r_k[l].reshape(-1), rwkv_ln_g[l], rwkv_ln_b[l], rwkv_k_k[l], rwkv_k_a[l])
        *seqs_p, gate_p, last_p = _rwkv_prep(proj, 0, bp, tp, jnp.zeros((bp, 1, P_RWKV_PAD), F32), *prep_args)
        *seqs_s, gate_s, last_s = _rwkv_prep(proj, mp, bs, ts, jnp.pad(state_rwkv_shift[l], ((0, 0), (0, 0), lane_pad)),
                                             *prep_args)
        outs["sh_p"].append(last_p[:, :, :P_RWKV])
        outs["sh_s"].append(last_s[:, :, :P_RWKV])
        scan_in_p = [_to_scan_layout(s, bp, tp) for s in seqs_p]
        scan_in_s = [_to_scan_layout(s, bs, ts) for s in seqs_s]

        wa_pad = jnp.pad(gla_wa2[l], ((0, LANE - GLA_LOWRANK), (0, 0))).astype(BF16)
        gla_args = (wa_pad, row1(gla_ba[l]), row1(gla_norm_g[l]))
        o_gla_p, sg_p = _gla(proj, 0, bp, tp, *gla_args, jnp.zeros((bp, GLA_HEADS, GLA_DV, GLA_DK), F32))
        o_gla_s, sg_s = _gla(proj, mp, bs, ts, *gla_args, state_gla[l].swapaxes(-1, -2))
        outs["gla_p"].append(sg_p.swapaxes(-1, -2))
        outs["gla_s"].append(sg_s.swapaxes(-1, -2))

        sbg = row1(sb_norm_g[l])
        o_sb_p = _sb_prompt(proj, bp, tp, sbg)
        o_sb_s = _sb_sample(proj, mp, bs, ts, l, cache_sb_k, cache_sb_v, sbg)

        z_p, sr_p = _rwkv_group(scan_in_p, bp, tp, vec_args, jnp.zeros((bp, RWKV_HEADS, RWKV_HEAD, RWKV_HEAD), F32))
        z_s, sr_s = _rwkv_group(scan_in_s, bs, ts, vec_args, state_rwkv[l])
        outs["rw_p"].append(sr_p)
        outs["rw_s"].append(sr_s)

        merged = _merge((o_gla_p, z_p, gate_p, o_sb_p), (o_gla_s, z_s, gate_s, o_sb_s))
        w_o = w_out[l].astype(BF16)
        x = _matmul_residual_joined(merged, w_o, *x_parts) if x is None else _matmul_residual(merged, w_o, x, tk=d)

        h2 = _rmsnorm(x, norm2_g[l], BF16)
        act = _swiglu(h2, _cast_pad_cols(w_gate, l, D_FF_PAD), _cast_pad_cols(w_up, l, D_FF_PAD))
        x = _matmul_residual_resident(act, _cast_pad_rows(w_down, l, D_FF_PAD), x)

    y_p, y_s = _rmsnorm_split(x, final_g, mp)
    sd = state_gla.dtype
    st = lambda name, dt=None: jnp.stack(outs[name]) if dt is None else jnp.stack(outs[name]).astype(dt)
    return (y_p.reshape(bp, tp, d), y_s.reshape(bs, ts, d),
            st("gla_p", sd), st("gla_s", sd), st("rw_p", sd), st("rw_s", sd),
            st("sh_p"), st("sh_s"), st("k_p"), st("k_s"), st("v_p"), st("v_s"))
```
